```python
import jax
import jax.numpy as jnp
from jax import lax
import numpy as np

D_MODEL = 1024
BATCH = 1
SEQ = 16384
DEPTH = 4

N_MIXERS = 3
ROPE_THETA = 500000.0
EPS = 1e-6
BLOCK = 128

A_HEADS = 16
A_KV_HEADS = 4
A_HEAD_DIM = D_MODEL // A_HEADS
A_ROT_DIM = A_HEAD_DIM // 4
A_WINDOW = 128

B_CONV_WIDTH = 3

C_HEADS = 16
C_NOPE_DIM = 64
C_ROPE_DIM = 32
C_V_DIM = 64
C_Q_RANK = 384
C_KV_RANK = 256

D_FF = ((8 * D_MODEL + 767) // 768) * 256

N_LAYERS_A = (DEPTH + N_MIXERS - 1) // N_MIXERS
N_LAYERS_B = (DEPTH + N_MIXERS - 2) // N_MIXERS
N_LAYERS_C = DEPTH // N_MIXERS

kernel_name = 'hybrid_swa_sink_shortconv_mla_swiglu'


def rms_norm(x, g):
    xf = x.astype(jnp.float32)
    y = xf * lax.rsqrt(jnp.mean(xf * xf, axis=-1, keepdims=True) + EPS)
    return (y * g.astype(jnp.float32)).astype(x.dtype)


def rotate(x, pos):
    r = x.shape[-1]
    inv_freq = ROPE_THETA ** (-jnp.arange(0, r, 2, dtype=jnp.float32) / r)
    ang = pos.astype(jnp.float32)[:, :, None] * inv_freq
    cos = jnp.cos(ang)[:, :, None, :]
    sin = jnp.sin(ang)[:, :, None, :]
    x1, x2 = jnp.split(x.astype(jnp.float32), 2, axis=-1)
    out = jnp.concatenate([x1 * cos - x2 * sin, x2 * cos + x1 * sin], axis=-1)
    return out.astype(x.dtype)


def swa_sink_attention(h, pos, w_qkv, q_norm, k_norm, sinks, w_o):
    b, s, _ = h.shape
    hq, hkv, hd = A_HEADS, A_KV_HEADS, A_HEAD_DIM
    grp = hq // hkv
    nblk = s // BLOCK
    q, k, v = jnp.split(h @ w_qkv, [hq * hd, (hq + hkv) * hd], axis=-1)
    q = rms_norm(q.reshape(b, s, hq, hd), q_norm)
    k = rms_norm(k.reshape(b, s, hkv, hd), k_norm)
    v = v.reshape(b, s, hkv, hd)
    q = jnp.concatenate([rotate(q[..., :A_ROT_DIM], pos), q[..., A_ROT_DIM:]], axis=-1)
    k = jnp.concatenate([rotate(k[..., :A_ROT_DIM], pos), k[..., A_ROT_DIM:]], axis=-1)
    qb = q.reshape(b, nblk, BLOCK, hkv, grp, hd)
    kb = k.reshape(b, nblk, BLOCK, hkv, hd)
    vb = v.reshape(b, nblk, BLOCK, hkv, hd)
    pad = ((0, 0), (1, 0), (0, 0), (0, 0), (0, 0))
    kw = jnp.concatenate([jnp.pad(kb, pad)[:, :-1], kb], axis=2)
    vw = jnp.concatenate([jnp.pad(vb, pad)[:, :-1], vb], axis=2)
    scores = jnp.einsum('bnqkgd,bnjkd->bnkgqj', qb, kw).astype(jnp.float32) * (hd ** -0.5)
    qi = jnp.arange(BLOCK)[:, None]
    kj = jnp.arange(2 * BLOCK)[None, :]
    delta = qi + BLOCK - kj
    band = (delta >= 0) & (delta < A_WINDOW)
    has_prev = (jnp.arange(nblk) > 0)[:, None, None] | (kj >= BLOCK)[None]
    valid = band[None] & has_prev
    scores = jnp.where(valid[None, :, None, None], scores, -jnp.inf)
    sink = sinks.astype(jnp.float32).reshape(hkv, grp)[None, None, :, :, None, None]
    m = jnp.maximum(jnp.max(scores, axis=-1, keepdims=True), sink)
    p = jnp.exp(scores - m)
    p = p / (jnp.sum(p, axis=-1, keepdims=True) + jnp.exp(sink - m))
    o = jnp.einsum('bnkgqj,bnjkd->bnqkgd', p.astype(vw.dtype), vw)
    return o.reshape(b, s, hq * hd) @ w_o


def short_conv_mixer(h, w_in, conv_w, w_out):
    s = h.shape[1]
    b_gate, c_gate, u = jnp.split(h @ w_in, 3, axis=-1)
    z = c_gate * u
    zp = jnp.pad(z, ((0, 0), (B_CONV_WIDTH - 1, 0), (0, 0)))
    y = sum(conv_w[i] * zp[:, i:i + s] for i in range(B_CONV_WIDTH))
    return (b_gate * y) @ w_out


def mla_attention(h, pos, w_down, q_a_norm, kv_a_norm, w_q_up, w_kv_up, q_norm, k_norm, w_o):
    b, s, _ = h.shape
    nh = C_HEADS
    dqk = C_NOPE_DIM + C_ROPE_DIM
    nblk = s // BLOCK
    cq, ckv, k_rope = jnp.split(h @ w_down, [C_Q_RANK, C_Q_RANK + C_KV_RANK], axis=-1)
    cq = rms_norm(cq, q_a_norm)
    ckv = rms_norm(ckv, kv_a_norm)
    q = (cq @ w_q_up).reshape(b, s, nh, dqk)
    k_nope, v = jnp.split((ckv @ w_kv_up).reshape(b, s, nh, C_NOPE_DIM + C_V_DIM), [C_NOPE_DIM], axis=-1)
    k = jnp.concatenate([k_nope, jnp.broadcast_to(k_rope[:, :, None, :], (b, s, nh, C_ROPE_DIM))], axis=-1)
    q = rms_norm(q, q_norm)
    k = rms_norm(k, k_norm)
    q = jnp.concatenate([q[..., :C_NOPE_DIM], rotate(q[..., C_NOPE_DIM:], pos)], axis=-1)
    k = jnp.concatenate([k[..., :C_NOPE_DIM], rotate(k[..., C_NOPE_DIM:], pos)], axis=-1)
    scale = dqk ** -0.5
    kpos = jnp.arange(s)
    qblocks = jnp.moveaxis(q.reshape(b, nblk, BLOCK, nh, dqk), 1, 0)

    def block_attn(args):
        qblk, j = args
        sc = jnp.einsum('bqhd,bkhd->bhqk', qblk, k).astype(jnp.float32) * scale
        qpos = j * BLOCK + jnp.arange(BLOCK)
        sc = jnp.where(kpos[None, :] <= qpos[:, None], sc, -jnp.inf)
        p = jax.nn.softmax(sc, axis=-1)
        return jnp.einsum('bhqk,bkhd->bqhd', p.astype(v.dtype), v)

    o = lax.map(block_attn, (qblocks, jnp.arange(nblk)))
    o = jnp.moveaxis(o, 0, 1).reshape(b, s, nh * C_V_DIM)
    return o @ w_o


def swiglu_ffn(h, w_gate_up, w_down):
    gate, up = jnp.split(h @ w_gate_up, 2, axis=-1)
    return (jax.nn.silu(gate) * up) @ w_down


def _normal(key, shape, scale):
    return jax.random.normal(key, shape, jnp.float32) * scale


def setup_inputs(seed: int = 0) -> dict:
    key = jax.random.key(seed)
    ks = jax.random.split(key, 24)
    d = D_MODEL
    out_scale = (2 * DEPTH) ** -0.5
    qkv_width = (A_HEADS + 2 * A_KV_HEADS) * A_HEAD_DIM
    dqk = C_NOPE_DIM + C_ROPE_DIM
    return {
        'x': _normal(ks[0], (BATCH, SEQ, d), 1.0),
        'positions': jax.random.randint(ks[1], (BATCH, 1), 0, 4096, dtype=jnp.int32) + jnp.arange(SEQ, dtype=jnp.int32)[None, :],
        'mix_norm': 1.0 + _normal(ks[2], (DEPTH, d), 0.02),
        'ffn_norm': 1.0 + _normal(ks[3], (DEPTH, d), 0.02),
        'a_w_qkv': _normal(ks[4], (N_LAYERS_A, d, qkv_width), d ** -0.5),
        'a_q_norm': 1.0 + _normal(ks[5], (N_LAYERS_A, A_HEAD_DIM), 0.02),
        'a_k_norm': 1.0 + _normal(ks[6], (N_LAYERS_A, A_HEAD_DIM), 0.02),
        'a_sinks': _normal(ks[7], (N_LAYERS_A, A_HEADS), 0.5),
        'a_w_o': _normal(ks[8], (N_LAYERS_A, A_HEADS * A_HEAD_DIM, d), (A_HEADS * A_HEAD_DIM) ** -0.5 * out_scale),
        'b_w_in': _normal(ks[9], (N_LAYERS_B, d, 3 * d), d ** -0.5),
        'b_conv_w': _normal(ks[10], (N_LAYERS_B, B_CONV_WIDTH, d), B_CONV_WIDTH ** -0.5),
        'b_w_out': _normal(ks[11], (N_LAYERS_B, d, d), d ** -0.5 * out_scale),
        'c_w_down': _normal(ks[12], (N_LAYERS_C, d, C_Q_RANK + C_KV_RANK + C_ROPE_DIM), d ** -0.5),
        'c_q_a_norm': 1.0 + _normal(ks[13], (N_LAYERS_C, C_Q_RANK), 0.02),
        'c_kv_a_norm': 1.0 + _normal(ks[14], (N_LAYERS_C, C_KV_RANK), 0.02),
        'c_w_q_up': _normal(ks[15], (N_LAYERS_C, C_Q_RANK, C_HEADS * dqk), C_Q_RANK ** -0.5),
        'c_w_kv_up': _normal(ks[16], (N_LAYERS_C, C_KV_RANK, C_HEADS * (C_NOPE_DIM + C_V_DIM)), C_KV_RANK ** -0.5),
        'c_q_norm': 1.0 + _normal(ks[17], (N_LAYERS_C, dqk), 0.02),
        'c_k_norm': 1.0 + _normal(ks[18], (N_LAYERS_C, dqk), 0.02),
        'c_w_o': _normal(ks[19], (N_LAYERS_C, C_HEADS * C_V_DIM, d), (C_HEADS * C_V_DIM) ** -0.5 * out_scale),
        'f_w_gate_up': _normal(ks[20], (DEPTH, d, 2 * D_FF), d ** -0.5),
        'f_w_down': _normal(ks[21], (DEPTH, D_FF, d), D_FF ** -0.5 * out_scale),
    }


def reference(x, positions, mix_norm, ffn_norm, a_w_qkv, a_q_norm, a_k_norm, a_sinks, a_w_o, b_w_in, b_conv_w, b_w_out, c_w_down, c_q_a_norm, c_kv_a_norm, c_w_q_up, c_w_kv_up, c_q_norm, c_k_norm, c_w_o, f_w_gate_up, f_w_down):
    for i in range(DEPTH):
        kind = i % N_MIXERS
        j = i // N_MIXERS
        h = rms_norm(x, mix_norm[i])
        if kind == 0:
            y = swa_sink_attention(h, positions, a_w_qkv[j], a_q_norm[j], a_k_norm[j], a_sinks[j], a_w_o[j])
        elif kind == 1:
            y = short_conv_mixer(h, b_w_in[j], b_conv_w[j], b_w_out[j])
        else:
            y = mla_attention(h, positions, c_w_down[j], c_q_a_norm[j], c_kv_a_norm[j], c_w_q_up[j], c_w_kv_up[j], c_q_norm[j], c_k_norm[j], c_w_o[j])
        x = x + y
        h = rms_norm(x, ffn_norm[i])
        x = x + swiglu_ffn(h, f_w_gate_up[i], f_w_down[i])
    return x
```

```python
import functools
import math

import jax
import jax.numpy as jnp
import numpy as np
from jax import lax
from jax.experimental import pallas as pl
from jax.experimental.pallas import tpu as pltpu

F32 = jnp.float32
BF16 = jnp.bfloat16

D_MODEL = 1024
ROPE_THETA = 500000.0
EPS = 1e-6
A_HEADS, A_KV_HEADS, A_HEAD_DIM, A_ROT_DIM, A_WINDOW = 16, 4, 64, 16, 128
C_HEADS, C_NOPE, C_ROPE, C_V, C_Q_RANK, C_KV_RANK = 16, 64, 32, 64, 384, 256
C_DQK = C_NOPE + C_ROPE
LOG2E = math.log2(math.e)

LANES = 128
VMEM_LIMIT_BYTES = 56 * 2**20

ROW_TILE = 512
ROPE_TILE = 2048
FFN_CHUNK = 256
MLA_TQ = 512
MLA_TK = 512


def _params(sem):
    return pltpu.CompilerParams(dimension_semantics=sem, vmem_limit_bytes=VMEM_LIMIT_BYTES)


def _const_spec(shape):
    return pl.BlockSpec(shape, lambda *_: (0,) * len(shape), pipeline_mode=pl.Buffered(1))


def _rms(x, g):
    ms = jnp.mean(x * x, axis=-1, keepdims=True)
    return x * lax.rsqrt(ms + EPS) * g


def _dot(a, b):
    return jnp.dot(a, b, preferred_element_type=F32)


def _dot_nt(a, b):
    return lax.dot_general(a, b, (((1,), (1,)), ((), ())), preferred_element_type=F32)


def _lane_lo():
    return lax.broadcasted_iota(jnp.int32, (1, LANES), 1) < (LANES // 2)


def _rope_apply(x, cos, sin, first_half, half):
    fwd = pltpu.roll(x, LANES - half, 1)
    bwd = pltpu.roll(x, half, 1)
    return x * cos + jnp.where(first_half, fwd, bwd) * sin


def _rope_kernel(pos_ref, invf_ref, sign_ref, cos_ref, sin_ref):
    ang = pos_ref[...].astype(F32) * invf_ref[...]
    cos_ref[...] = jnp.cos(ang)
    sin_ref[...] = jnp.sin(ang) * sign_ref[...]


def _rope_tables(pos_col, invf, sign):
    s = pos_col.shape[0]
    row = pl.BlockSpec((1, LANES), lambda i: (0, 0))
    out = pl.BlockSpec((ROPE_TILE, LANES), lambda i: (i, 0))
    return pl.pallas_call(
        _rope_kernel,
        out_shape=(jax.ShapeDtypeStruct((s, LANES), F32),) * 2,
        grid=(s // ROPE_TILE,),
        in_specs=[pl.BlockSpec((ROPE_TILE, 1), lambda i: (i, 0)), row, row],
        out_specs=(out, out),
        compiler_params=_params(("parallel",)),
        name="rope_tables",
    )(pos_col, invf, sign)


def _rope_pattern(rot_dim, offset, period):
    half = rot_dim // 2
    inv_freq = ROPE_THETA ** (-jnp.arange(0, rot_dim, 2, dtype=F32) / rot_dim)
    d = np.arange(LANES) % period - offset
    rotary = (d >= 0) & (d < rot_dim)
    idx = np.where(rotary, d % half, 0)
    invf = jnp.where(jnp.asarray(rotary), inv_freq[idx], 0.0)
    sign = np.where(rotary, np.where(d < half, -1.0, 1.0), 0.0)
    return invf.reshape(1, LANES).astype(F32), jnp.asarray(sign, F32).reshape(1, LANES)


def _a_proj_kernel(x_ref, g_ref, w_ref, qn_ref, kn_ref, cos_ref, sin_ref, sign_ref, q_ref, k_ref, v_ref):
    h = _rms(x_ref[...], g_ref[...]).astype(BF16)
    qkv = _dot(h, w_ref[...])
    lo = _lane_lo()
    first = sign_ref[...] < 0.0
    cos, sin = cos_ref[...], sin_ref[...]
    nq = A_HEADS * A_HEAD_DIM
    nk = 2 * A_KV_HEADS * LANES
    q_scale = A_HEAD_DIM ** -0.5 * LOG2E
    for c in range(nq // LANES):
        col = qkv[:, c * LANES:(c + 1) * LANES]
        sq = col * col
        s_lo = jnp.sum(jnp.where(lo, sq, 0.0), axis=-1, keepdims=True)
        s_hi = jnp.sum(jnp.where(lo, 0.0, sq), axis=-1, keepdims=True)
        ms = jnp.where(lo, s_lo, s_hi) * (1.0 / A_HEAD_DIM)
        col = col * lax.rsqrt(ms + EPS) * qn_ref[...]
        col = _rope_apply(col, cos, sin, first, A_ROT_DIM // 2) * q_scale
        q_ref[:, c * LANES:(c + 1) * LANES] = col.astype(BF16)
    for c in range(2 * A_KV_HEADS):
        col = qkv[:, nq + c * LANES:nq + (c + 1) * LANES]
        ms = jnp.sum(col * col, axis=-1, keepdims=True) * (1.0 / A_HEAD_DIM)
        col = col * lax.rsqrt(ms + EPS) * kn_ref[...]
        col = _rope_apply(col, cos, sin, first, A_ROT_DIM // 2)
        k_ref[:, c * LANES:(c + 1) * LANES] = col.astype(BF16)
    v_ref[...] = qkv[:, nq + nk:].astype(BF16)


def _a_proj(x, g, w, qn, kn, cos, sin, sign):
    s = x.shape[0]
    nq, nk, nv = A_HEADS * A_HEAD_DIM, 2 * A_KV_HEADS * LANES, A_KV_HEADS * LANES
    row = lambda n: pl.BlockSpec((ROW_TILE, n), lambda i: (i, 0))
    return pl.pallas_call(
        _a_proj_kernel,
        out_shape=(jax.ShapeDtypeStruct((s, nq), BF16), jax.ShapeDtypeStruct((s, nk), BF16),
                   jax.ShapeDtypeStruct((s, nv), BF16)),
        grid=(s // ROW_TILE,),
        in_specs=[row(D_MODEL), _const_spec((1, D_MODEL)), _const_spec(w.shape), _const_spec((1, LANES)),
                  _const_spec((1, LANES)), row(LANES), row(LANES), _const_spec((1, LANES))],
        out_specs=(row(nq), row(nk), row(nv)),
        compiler_params=_params(("parallel",)),
        name="swa_qkv_proj",
    )(x, g, w, qn, kn, cos, sin, sign)


def _a_attn_kernel(sink_ref, q_ref, kp_ref, kc_ref, vp_ref, vc_ref, x_ref, wo_ref, o_ref, oacc_ref):
    i = pl.program_id(0)
    blk = A_WINDOW
    kcat = jnp.concatenate([kp_ref[...], kc_ref[...]], axis=0)
    vcat = jnp.concatenate([vp_ref[...], vc_ref[...]], axis=0)
    lo = _lane_lo()
    qi = lax.broadcasted_iota(jnp.int32, (2 * blk, 2 * blk), 0) & (blk - 1)
    kj = lax.broadcasted_iota(jnp.int32, (2 * blk, 2 * blk), 1)
    first_key = jnp.where(i > 0, 0, blk)
    for b in range(ROW_TILE // blk):
        valid = (kj > qi) & (kj <= qi + blk)
        if b == 0:
            valid = valid & (kj >= first_key)
        rows = slice(b * blk, (b + 1) * blk)
        for g in range(A_KV_HEADS):
            qs = jnp.concatenate([q_ref[rows, (2 * g) * LANES:(2 * g + 1) * LANES],
                                  q_ref[rows, (2 * g + 1) * LANES:(2 * g + 2) * LANES]], axis=0)
            vd = vcat[b * blk:(b + 2) * blk, g * LANES:(g + 1) * LANES]
            outs = []
            for half in range(2):
                kd = kcat[b * blk:(b + 2) * blk, (2 * g + half) * LANES:(2 * g + half + 1) * LANES]
                sink = jnp.concatenate([jnp.full((blk, 1), sink_ref[4 * g + half] * LOG2E, F32),
                                        jnp.full((blk, 1), sink_ref[4 * g + 2 + half] * LOG2E, F32)], axis=0)
                s = jnp.where(valid, _dot_nt(qs, kd), -jnp.inf)
                m = jnp.maximum(jnp.max(s, axis=-1, keepdims=True), sink)
                p = jnp.exp2(s - m)
                denom = jnp.sum(p, axis=-1, keepdims=True) + jnp.exp2(sink - m)
                outs.append(_dot(p.astype(BF16), vd) / denom)
            pair = jnp.where(lo, outs[0], outs[1])
            oacc_ref[rows, (2 * g) * LANES:(2 * g + 1) * LANES] = pair[:blk].astype(BF16)
            oacc_ref[rows, (2 * g + 1) * LANES:(2 * g + 2) * LANES] = pair[blk:].astype(BF16)
    o_ref[...] = x_ref[...] + _dot(oacc_ref[...], wo_ref[...])


def _a_attn(sinks, q, k, v, x, wo):
    s = x.shape[0]
    nk, nv = 2 * A_KV_HEADS * LANES, A_KV_HEADS * LANES
    per = ROW_TILE // A_WINDOW
    row = lambda n: pl.BlockSpec((ROW_TILE, n), lambda i: (i, 0))
    prev = lambda n: pl.BlockSpec((A_WINDOW, n), lambda i: (jnp.maximum(i * per - 1, 0), 0))
    return pl.pallas_call(
        _a_attn_kernel,
        out_shape=jax.ShapeDtypeStruct((s, D_MODEL), F32),
        grid=(s // ROW_TILE,),
        in_specs=[pl.BlockSpec(memory_space=pltpu.SMEM), row(A_HEADS * A_HEAD_DIM), prev(nk), row(nk), prev(nv),
                  row(nv), row(D_MODEL), _const_spec(wo.shape)],
        out_specs=row(D_MODEL),
        scratch_shapes=[pltpu.VMEM((ROW_TILE, A_HEADS * A_HEAD_DIM), BF16)],
        compiler_params=_params(("parallel",)),
        name="swa_attention",
    )(sinks, q, k, k, v, v, x, wo)


def _mixer_a(x, g, w_qkv, q_norm, k_norm, sinks, w_o, tables):
    cos, sin, sign = tables
    nq = A_HEADS * A_HEAD_DIM
    nkv = A_KV_HEADS * A_HEAD_DIM
    wq = w_qkv[:, :nq]
    wk = w_qkv[:, nq:nq + nkv].reshape(D_MODEL, A_KV_HEADS, 1, A_HEAD_DIM)
    wv = w_qkv[:, nq + nkv:].reshape(D_MODEL, A_KV_HEADS, 1, A_HEAD_DIM)
    zero = jnp.zeros_like(wk)
    wk = jnp.concatenate([wk, zero, zero, wk], axis=2).reshape(D_MODEL, 2 * A_KV_HEADS * LANES)
    wv = jnp.concatenate([wv, wv], axis=2).reshape(D_MODEL, A_KV_HEADS * LANES)
    w = jnp.concatenate([wq, wk, wv], axis=1).astype(BF16)
    qn = jnp.tile(q_norm, 2).reshape(1, LANES)
    kn = jnp.tile(k_norm, 2).reshape(1, LANES)
    q, k, v = _a_proj(x, g.reshape(1, -1), w, qn, kn, cos, sin, sign)
    return _a_attn(sinks, q, k, v, x, w_o.astype(BF16))


def _b_kernel(x_ref, g_ref, win_ref, cw_ref, wout_ref, o_ref, carry_ref):
    @pl.when(pl.program_id(0) == 0)
    def _():
        carry_ref[...] = jnp.zeros_like(carry_ref)

    x = x_ref[...]
    h = _rms(x, g_ref[...]).astype(BF16)
    bcu = _dot(h, win_ref[...])
    z = bcu[:, D_MODEL:2 * D_MODEL] * bcu[:, 2 * D_MODEL:]
    r = lax.broadcasted_iota(jnp.int32, (ROW_TILE, 1), 0)
    z1 = jnp.where(r == 0, carry_ref[7:8, :], pltpu.roll(z, 1, 0))
    z2 = jnp.where(r == 0, carry_ref[6:7, :], jnp.where(r == 1, carry_ref[7:8, :], pltpu.roll(z, 2, 0)))
    y = cw_ref[0:1, :] * z2 + cw_ref[1:2, :] * z1 + cw_ref[2:3, :] * z
    carry_ref[...] = z[ROW_TILE - 8:]
    o_ref[...] = x + _dot((bcu[:, :D_MODEL] * y).astype(BF16), wout_ref[...])


def _mixer_b(x, g, w_in, conv_w, w_out):
    s = x.shape[0]
    row = pl.BlockSpec((ROW_TILE, D_MODEL), lambda i: (i, 0))
    return pl.pallas_call(
        _b_kernel,
        out_shape=jax.ShapeDtypeStruct((s, D_MODEL), F32),
        grid=(s // ROW_TILE,),
        in_specs=[row, _const_spec((1, D_MODEL)), _const_spec(w_in.shape), _const_spec(conv_w.shape),
                  _const_spec(w_out.shape)],
        out_specs=row,
        scratch_shapes=[pltpu.VMEM((8, D_MODEL), F32)],
        compiler_params=_params(("arbitrary",)),
        name="short_conv",
    )(x, g.reshape(1, -1), w_in.astype(BF16), conv_w, w_out.astype(BF16))


def _c_proj_kernel(x_ref, g_ref, wd_ref, qan_ref, kvan_ref, wq_ref, wkv_ref, qn_ref, kn_ref, cos_ref, sin_ref,
                   sign_ref, q_ref, k_ref, v_ref):
    h = _rms(x_ref[...], g_ref[...]).astype(BF16)
    d = _dot(h, wd_ref[...])
    cq = _rms(d[:, :C_Q_RANK], qan_ref[...]).astype(BF16)
    ckv = _rms(d[:, C_Q_RANK:C_Q_RANK + C_KV_RANK], kvan_ref[...]).astype(BF16)
    k_rope = pltpu.roll(d[:, C_Q_RANK + C_KV_RANK:], C_NOPE, 1)
    q_all = _dot(cq, wq_ref[...])
    kv_all = _dot(ckv, wkv_ref[...])
    first = sign_ref[...] < 0.0
    cos, sin = cos_ref[...], sin_ref[...]
    q_scale = C_DQK ** -0.5 * LOG2E

    def head(col, gain):
        ms = jnp.sum(col * col, axis=-1, keepdims=True) * (1.0 / C_DQK)
        col = col * lax.rsqrt(ms + EPS) * gain
        return _rope_apply(col, cos, sin, first, C_ROPE // 2)

    for hh in range(C_HEADS):
        cols = slice(hh * LANES, (hh + 1) * LANES)
        q_ref[hh] = (head(q_all[:, cols], qn_ref[...]) * q_scale).astype(BF16)
        k_ref[hh] = head(kv_all[:, cols] + k_rope, kn_ref[...]).astype(BF16)
    for j in range(C_HEADS // 2):
        v_ref[j] = kv_all[:, (C_HEADS + j) * LANES:(C_HEADS + j + 1) * LANES].astype(BF16)


def _c_proj(x, g, wd, qan, kvan, wq, wkv, qn, kn, cos, sin, sign):
    s = x.shape[0]
    row = lambda n: pl.BlockSpec((ROW_TILE, n), lambda i: (i, 0))
    heads = lambda n: pl.BlockSpec((n, ROW_TILE, LANES), lambda i: (0, i, 0))
    return pl.pallas_call(
        _c_proj_kernel,
        out_shape=(jax.ShapeDtypeStruct((C_HEADS, s, LANES), BF16), jax.ShapeDtypeStruct((C_HEADS, s, LANES), BF16),
                   jax.ShapeDtypeStruct((C_HEADS // 2, s, LANES), BF16)),
        grid=(s // ROW_TILE,),
        in_specs=[row(D_MODEL), _const_spec((1, D_MODEL)), _const_spec(wd.shape), _const_spec(qan.shape),
                  _const_spec(kvan.shape), _const_spec(wq.shape), _const_spec(wkv.shape), _const_spec((1, LANES)),
                  _const_spec((1, LANES)), row(LANES), row(LANES), _const_spec((1, LANES))],
        out_specs=(heads(C_HEADS), heads(C_HEADS), heads(C_HEADS // 2)),
        compiler_params=_params(("parallel",)),
        name="mla_proj",
    )(x, g, wd, qan, kvan, wq, wkv, qn, kn, cos, sin, sign)


def _c_flash_kernel(qi_ref, ki_ref, last_ref, q_ref, k_ref, v_ref, x_ref, wo_ref, o_ref, m_ref, l_ref, acc_ref):
    step = pl.program_id(0)
    qi, ki = qi_ref[step], ki_ref[step]
    lo = _lane_lo()

    @pl.when(ki == 0)
    def _():
        m_ref[...] = jnp.full_like(m_ref, -jnp.inf)
        l_ref[...] = jnp.zeros_like(l_ref)
        acc_ref[...] = jnp.zeros_like(acc_ref)

    def sweep(masked):
        if masked:
            qpos = qi * MLA_TQ + lax.broadcasted_iota(jnp.int32, (MLA_TQ, MLA_TK), 0)
            kpos = ki * MLA_TK + lax.broadcasted_iota(jnp.int32, (MLA_TQ, MLA_TK), 1)
            causal = kpos <= qpos

        def pair(j, carry):
            vp = v_ref[j]
            alphas, pvs = [], []
            for t in range(2):
                hh = 2 * j + t
                s = _dot_nt(q_ref[hh], k_ref[hh])
                if masked:
                    s = jnp.where(causal, s, -jnp.inf)
                m_prev = m_ref[hh]
                m_new = jnp.maximum(m_prev, jnp.max(s, axis=-1, keepdims=True))
                alpha = jnp.exp2(m_prev - m_new)
                p = jnp.exp2(s - m_new)
                l_ref[hh] = alpha * l_ref[hh] + jnp.sum(p, axis=-1, keepdims=True)
                m_ref[hh] = m_new
                alphas.append(alpha)
                pvs.append(_dot(p.astype(BF16), vp))
            acc_ref[j] = acc_ref[j] * jnp.where(lo, alphas[0], alphas[1]) + jnp.where(lo, pvs[0], pvs[1])
            return carry

        lax.fori_loop(0, C_HEADS // 2, pair, 0)

    needs_mask = (ki + 1) * MLA_TK - 1 > qi * MLA_TQ

    @pl.when(needs_mask)
    def _():
        sweep(True)

    @pl.when(jnp.logical_not(needs_mask))
    def _():
        sweep(False)

    @pl.when(last_ref[step] == 1)
    def _():
        cols = []
        for j in range(C_HEADS // 2):
            cols.append((acc_ref[j] / jnp.where(lo, l_ref[2 * j], l_ref[2 * j + 1])).astype(BF16))
        o_ref[...] = x_ref[...] + _dot(jnp.concatenate(cols, axis=1), wo_ref[...])


def _flash_schedule(s):
    qi, ki, last = [], [], []
    for a in range(s // MLA_TQ):
        nk = -(-((a + 1) * MLA_TQ) // MLA_TK)
        for b in range(nk):
            qi.append(a)
            ki.append(b)
            last.append(int(b == nk - 1))
    return tuple(jnp.asarray(v, jnp.int32) for v in (qi, ki, last))


def _c_flash(q, k, v, x, wo):
    s = x.shape[0]
    qi, ki, last = _flash_schedule(s)
    grid_spec = pltpu.PrefetchScalarGridSpec(
        num_scalar_prefetch=3,
        grid=(int(qi.shape[0]),),
        in_specs=[
            pl.BlockSpec((C_HEADS, MLA_TQ, LANES), lambda t, qi, ki, last: (0, qi[t], 0)),
            pl.BlockSpec((C_HEADS, MLA_TK, LANES), lambda t, qi, ki, last: (0, ki[t], 0)),
            pl.BlockSpec((C_HEADS // 2, MLA_TK, LANES), lambda t, qi, ki, last: (0, ki[t], 0)),
            pl.BlockSpec((MLA_TQ, D_MODEL), lambda t, qi, ki, last: (qi[t], 0)),
            pl.BlockSpec(wo.shape, lambda t, qi, ki, last: (0, 0), pipeline_mode=pl.Buffered(1)),
        ],
        out_specs=pl.BlockSpec((MLA_TQ, D_MODEL), lambda t, qi, ki, last: (qi[t], 0)),
        scratch_shapes=[pltpu.VMEM((C_HEADS, MLA_TQ, 1), F32), pltpu.VMEM((C_HEADS, MLA_TQ, 1), F32),
                        pltpu.VMEM((C_HEADS // 2, MLA_TQ, LANES), F32)],
    )
    return pl.pallas_call(
        _c_flash_kernel,
        out_shape=jax.ShapeDtypeStruct((s, D_MODEL), F32),
        grid_spec=grid_spec,
        compiler_params=_params(("arbitrary",)),
        name="mla_flash",
    )(qi, ki, last, q, k, v, x, wo)


def _pad_lanes(v, n=LANES):
    return jnp.pad(v, (0, n - v.shape[0])).reshape(1, n)


def _mixer_c(x, g, w_down, q_a_norm, kv_a_norm, w_q_up, w_kv_up, q_norm, k_norm, w_o, tables):
    cos, sin, sign = tables
    wd = jnp.pad(w_down, ((0, 0), (0, 6 * LANES - w_down.shape[1]))).astype(BF16)
    wq = jnp.pad(w_q_up.reshape(C_Q_RANK, C_HEADS, C_DQK), ((0, 0), (0, 0), (0, LANES - C_DQK)))
    wq = wq.reshape(C_Q_RANK, C_HEADS * LANES).astype(BF16)
    wkv = w_kv_up.reshape(C_KV_RANK, C_HEADS, C_NOPE + C_V)
    wk = jnp.pad(wkv[:, :, :C_NOPE], ((0, 0), (0, 0), (0, LANES - C_NOPE))).reshape(C_KV_RANK, C_HEADS * LANES)
    wv = wkv[:, :, C_NOPE:].reshape(C_KV_RANK, C_HEADS * C_V)
    wkv = jnp.concatenate([wk, wv], axis=1).astype(BF16)
    q, k, v = _c_proj(x, g.reshape(1, -1), wd, q_a_norm.reshape(1, -1), kv_a_norm.reshape(1, -1), wq, wkv,
                      _pad_lanes(q_norm), _pad_lanes(k_norm), cos, sin, sign)
    return _c_flash(q, k, v, x, w_o.astype(BF16))


def _ffn_kernel(x_ref, g_ref, wgu_ref, wd_ref, o_ref):
    x = x_ref[...]
    h = _rms(x, g_ref[...]).astype(BF16)
    d_ff = wd_ref.shape[0]
    acc = x
    for c in range(d_ff // FFN_CHUNK):
        c0 = c * FFN_CHUNK
        gate = _dot(h, wgu_ref[:, c0:c0 + FFN_CHUNK])
        up = _dot(h, wgu_ref[:, d_ff + c0:d_ff + c0 + FFN_CHUNK])
        act = (gate * jax.nn.sigmoid(gate) * up).astype(BF16)
        acc = acc + _dot(act, wd_ref[c0:c0 + FFN_CHUNK, :])
    o_ref[...] = acc


def _ffn(x, g, w_gate_up, w_down):
    s = x.shape[0]
    row = pl.BlockSpec((ROW_TILE, D_MODEL), lambda i: (i, 0))
    return pl.pallas_call(
        _ffn_kernel,
        out_shape=jax.ShapeDtypeStruct((s, D_MODEL), F32),
        grid=(s // ROW_TILE,),
        in_specs=[row, _const_spec((1, D_MODEL)), _const_spec(w_gate_up.shape), _const_spec(w_down.shape)],
        out_specs=row,
        compiler_params=_params(("parallel",)),
        name="swiglu_ffn",
    )(x, g.reshape(1, -1), w_gate_up.astype(BF16), w_down.astype(BF16))


def kernel(x, positions, mix_norm, ffn_norm, a_w_qkv, a_q_norm, a_k_norm, a_sinks, a_w_o, b_w_in, b_conv_w, b_w_out,
           c_w_down, c_q_a_norm, c_kv_a_norm, c_w_q_up, c_w_kv_up, c_q_norm, c_k_norm, c_w_o, f_w_gate_up, f_w_down):
    batch, seq, d = x.shape
    assert batch == 1 and d == D_MODEL and seq % max(ROW_TILE, ROPE_TILE, MLA_TQ, MLA_TK) == 0
    depth = mix_norm.shape[0]
    pos_col = positions.reshape(seq, 1)
    pat_a = _rope_pattern(A_ROT_DIM, 0, A_HEAD_DIM)
    pat_c = _rope_pattern(C_ROPE, C_NOPE, LANES)
    tab_a = _rope_tables(pos_col, *pat_a) + (pat_a[1],)
    tab_c = _rope_tables(pos_col, *pat_c) + (pat_c[1],)
    xs = x.reshape(seq, d)
    for i in range(depth):
        kind, j = i % 3, i // 3
        if kind == 0:
            xs = _mixer_a(xs, mix_norm[i], a_w_qkv[j], a_q_norm[j], a_k_norm[j], a_sinks[j], a_w_o[j], tab_a)
        elif kind == 1:
            xs = _mixer_b(xs, mix_norm[i], b_w_in[j], b_conv_w[j], b_w_out[j])
        else:
            xs = _mixer_c(xs, mix_norm[i], c_w_down[j], c_q_a_norm[j], c_kv_a_norm[j], c_w_q_up[j], c_w_kv_up[j],
                          c_q_norm[j], c_k_norm[j], c_w_o[j], tab_c)
        xs = _ffn(xs, ffn_norm[i], f_w_gate_up[i], f_w_down[i])
    return xs.reshape(batch, seq, d)
```

```python
import functools
import math

import jax
import jax.numpy as jnp
import numpy as np
from jax import lax
from jax.experimental import pallas as pl
from jax.experimental.pallas import tpu as pltpu

F32 = jnp.float32
BF16 = jnp.bfloat16

D_MODEL = 1024
ROPE_THETA = 500000.0
EPS = 1e-6
A_HEADS, A_KV_HEADS, A_HEAD_DIM, A_ROT_DIM, A_WINDOW = 16, 4, 64, 16, 128
C_HEADS, C_NOPE, C_ROPE, C_V, C_Q_RANK, C_KV_RANK = 16, 64, 32, 64, 384, 256
C_DQK = C_NOPE + C_ROPE
LOG2E = math.log2(math.e)

LANES = 128
VMEM_LIMIT_BYTES = 56 * 2**20

ROW_TILE = 512
ROPE_TILE = 2048
FFN_CHUNK = 256
MLA_TQ = 512
MLA_TK = 512


def _params(sem):
    return pltpu.CompilerParams(dimension_semantics=sem, vmem_limit_bytes=VMEM_LIMIT_BYTES)


def _const_spec(shape):
    return pl.BlockSpec(shape, lambda *_: (0,) * len(shape), pipeline_mode=pl.Buffered(1))


def _rms(x, g):
    ms = jnp.mean(x * x, axis=-1, keepdims=True)
    return x * lax.rsqrt(ms + EPS) * g


def _dot(a, b):
    return jnp.dot(a, b, preferred_element_type=F32)


def _dot_nt(a, b):
    return lax.dot_general(a, b, (((1,), (1,)), ((), ())), preferred_element_type=F32)


def _lane_lo():
    return lax.broadcasted_iota(jnp.int32, (1, LANES), 1) < (LANES // 2)


def _rope_apply(x, cos, sin, first_half, half):
    fwd = pltpu.roll(x, LANES - half, 1)
    bwd = pltpu.roll(x, half, 1)
    return x * cos + jnp.where(first_half, fwd, bwd) * sin


def _rope_kernel(pos_ref, invf_ref, sign_ref, cos_ref, sin_ref):
    ang = pos_ref[...].astype(F32) * invf_ref[...]
    cos_ref[...] = jnp.cos(ang)
    sin_ref[...] = jnp.sin(ang) * sign_ref[...]


def _rope_tables(pos_col, invf, sign):
    s = pos_col.shape[0]
    row = pl.BlockSpec((1, LANES), lambda i: (0, 0))
    out = pl.BlockSpec((ROPE_TILE, LANES), lambda i: (i, 0))
    return pl.pallas_call(
        _rope_kernel,
        out_shape=(jax.ShapeDtypeStruct((s, LANES), F32),) * 2,
        grid=(s // ROPE_TILE,),
        in_specs=[pl.BlockSpec((ROPE_TILE, 1), lambda i: (i, 0)), row, row],
        out_specs=(out, out),
        compiler_params=_params(("parallel",)),
        name="rope_tables",
    )(pos_col, invf, sign)


def _rope_pattern(rot_dim, offset, period):
    half = rot_dim // 2
    inv_freq = ROPE_THETA ** (-jnp.arange(0, rot_dim, 2, dtype=F32) / rot_dim)
    d = np.arange(LANES) % period - offset
    rotary = (d >= 0) & (d < rot_dim)
    idx = np.where(rotary, d % half, 0)
    invf = jnp.where(jnp.asarray(rotary), inv_freq[idx], 0.0)
    sign = np.where(rotary, np.where(d < half, -1.0, 1.0), 0.0)
    return invf.reshape(1, LANES).astype(F32), jnp.asarray(sign, F32).reshape(1, LANES)


def _a_proj_kernel(x_ref, g_ref, w_ref, qn_ref, kn_ref, cos_ref, sin_ref, sign_ref, q_ref, k_ref, v_ref):
    h = _rms(x_ref[...], g_ref[...]).astype(BF16)
    qkv = _dot(h, w_ref[...])
    lo = _lane_lo()
    first = sign_ref[...] < 0.0
    cos, sin = cos_ref[...], sin_ref[...]
    nq = A_HEADS * A_HEAD_DIM
    nk = 2 * A_KV_HEADS * LANES
    q_scale = A_HEAD_DIM ** -0.5 * LOG2E
    for c in range(nq // LANES):
        col = qkv[:, c * LANES:(c + 1) * LANES]
        sq = col * col
        s_lo = jnp.sum(jnp.where(lo, sq, 0.0), axis=-1, keepdims=True)
        s_hi = jnp.sum(jnp.where(lo, 0.0, sq), axis=-1, keepdims=True)
        ms = jnp.where(lo, s_lo, s_hi) * (1.0 / A_HEAD_DIM)
        col = col * lax.rsqrt(ms + EPS) * qn_ref[...]
        col = _rope_apply(col, cos, sin, first, A_ROT_DIM // 2) * q_scale
        q_ref[:, c * LANES:(c + 1) * LANES] = col.astype(BF16)
    for c in range(2 * A_KV_HEADS):
        col = qkv[:, nq + c * LANES:nq + (c + 1) * LANES]
        ms = jnp.sum(col * col, axis=-1, keepdims=True) * (1.0 / A_HEAD_DIM)
        col = col * lax.rsqrt(ms + EPS) * kn_ref[...]
        col = _rope_apply(col, cos, sin, first, A_ROT_DIM // 2)
        k_ref[:, c * LANES:(c + 1) * LANES] = col.astype(BF16)
    v_ref[...] = qkv[:, nq + nk:].astype(BF16)


def _a_proj(x, g, w, qn, kn, cos, sin, sign):
    s = x.shape[0]
    nq, nk, nv = A_HEADS * A_HEAD_DIM, 2 * A_KV_HEADS * LANES, A_KV_HEADS * LANES
    row = lambda n: pl.BlockSpec((ROW_TILE, n), lambda i: (i, 0))
    return pl.pallas_call(
        _a_proj_kernel,
        out_shape=(jax.ShapeDtypeStruct((s, nq), BF16), jax.ShapeDtypeStruct((s, nk), BF16),
                   jax.ShapeDtypeStruct((s, nv), BF16)),
        grid=(s // ROW_TILE,),
        in_specs=[row(D_MODEL), _const_spec((1, D_MODEL)), _const_spec(w.shape), _const_spec((1, LANES)),
                  _const_spec((1, LANES)), row(LANES), row(LANES), _const_spec((1, LANES))],
        out_specs=(row(nq), row(nk), row(nv)),
        compiler_params=_params(("parallel",)),
        name="swa_qkv_proj",
    )(x, g, w, qn, kn, cos, sin, sign)


def _a_attn_kernel(sink_ref, q_ref, kp_ref, kc_ref, vp_ref, vc_ref, x_ref, wo_ref, o_ref, oacc_ref):
    i = pl.program_id(0)
    blk = A_WINDOW
    kcat = jnp.concatenate([kp_ref[...], kc_ref[...]], axis=0)
    vcat = jnp.concatenate([vp_ref[...], vc_ref[...]], axis=0)
    lo = _lane_lo()
    qi = lax.broadcasted_iota(jnp.int32, (2 * blk, 2 * blk), 0) & (blk - 1)
    kj = lax.broadcasted_iota(jnp.int32, (2 * blk, 2 * blk), 1)
    first_key = jnp.where(i > 0, 0, blk)
    for b in range(ROW_TILE // blk):
        valid = (kj > qi) & (kj <= qi + blk)
        if b == 0:
            valid = valid & (kj >= first_key)
        rows = slice(b * blk, (b + 1) * blk)
        for g in range(A_KV_HEADS):
            qs = jnp.concatenate([q_ref[rows, (2 * g) * LANES:(2 * g + 1) * LANES],
                                  q_ref[rows, (2 * g + 1) * LANES:(2 * g + 2) * LANES]], axis=0)
            vd = vcat[b * blk:(b + 2) * blk, g * LANES:(g + 1) * LANES]
            outs = []
            for half in range(2):
                kd = kcat[b * blk:(b + 2) * blk, (2 * g + half) * LANES:(2 * g + half + 1) * LANES]
                sink = jnp.concatenate([jnp.full((blk, 1), sink_ref[4 * g + half] * LOG2E, F32),
                                        jnp.full((blk, 1), sink_ref[4 * g + 2 + half] * LOG2E, F32)], axis=0)
                s = jnp.where(valid, _dot_nt(qs, kd), -jnp.inf)
                m = jnp.maximum(jnp.max(s, axis=-1, keepdims=True), sink)
                p = jnp.exp2(s - m)
                denom = jnp.sum(p, axis=-1, keepdims=True) + jnp.exp2(sink - m)
                outs.append(_dot(p.astype(BF16), vd) / denom)
            pair = jnp.where(lo, outs[0], outs[1])
            oacc_ref[rows, (2 * g) * LANES:(2 * g + 1) * LANES] = pair[:blk].astype(BF16)
            oacc_ref[rows, (2 * g + 1) * LANES:(2 * g + 2) * LANES] = pair[blk:].astype(BF16)
    o_ref[...] = x_ref[...] + _dot(oacc_ref[...], wo_ref[...])


def _a_attn(sinks, q, k, v, x, wo):
    s = x.shape[0]
    nk, nv = 2 * A_KV_HEADS * LANES, A_KV_HEADS * LANES
    per = ROW_TILE // A_WINDOW
    row = lambda n: pl.BlockSpec((ROW_TILE, n), lambda i: (i, 0))
    prev = lambda n: pl.BlockSpec((A_WINDOW, n), lambda i: (jnp.maximum(i * per - 1, 0), 0))
    return pl.pallas_call(
        _a_attn_kernel,
        out_shape=jax.ShapeDtypeStruct((s, D_MODEL), F32),
        grid=(s // ROW_TILE,),
        in_specs=[pl.BlockSpec(memory_space=pltpu.SMEM), row(A_HEADS * A_HEAD_DIM), prev(nk), row(nk), prev(nv),
                  row(nv), row(D_MODEL), _const_spec(wo.shape)],
        out_specs=row(D_MODEL),
        scratch_shapes=[pltpu.VMEM((ROW_TILE, A_HEADS * A_HEAD_DIM), BF16)],
        compiler_params=_params(("parallel",)),
        name="swa_attention",
    )(sinks, q, k, k, v, v, x, wo)


def _mixer_a(x, g, w_qkv, q_norm, k_norm, sinks, w_o, tables):
    cos, sin, sign = tables
    nq = A_HEADS * A_HEAD_DIM
    nkv = A_KV_HEADS * A_HEAD_DIM
    wq = w_qkv[:, :nq]
    wk = w_qkv[:, nq:nq + nkv].reshape(D_MODEL, A_KV_HEADS, 1, A_HEAD_DIM)
    wv = w_qkv[:, nq + nkv:].reshape(D_MODEL, A_KV_HEADS, 1, A_HEAD_DIM)
    zero = jnp.zeros_like(wk)
    wk = jnp.concatenate([wk, zero, zero, wk], axis=2).reshape(D_MODEL, 2 * A_KV_HEADS * LANES)
    wv = jnp.concatenate([wv, wv], axis=2).reshape(D_MODEL, A_KV_HEADS * LANES)
    w = jnp.concatenate([wq, wk, wv], axis=1).astype(BF16)
    qn = jnp.tile(q_norm, 2).reshape(1, LANES)
    kn = jnp.tile(k_norm, 2).reshape(1, LANES)
    q, k, v = _a_proj(x, g.reshape(1, -1), w, qn, kn, cos, sin, sign)
    return _a_attn(sinks, q, k, v, x, w_o.astype(BF16))


def _b_kernel(x_ref, g_ref, win_ref, cw_ref, wout_ref, o_ref, carry_ref):
    @pl.when(pl.program_id(0) == 0)
    def _():
        carry_ref[...] = jnp.zeros_like(carry_ref)

    x = x_ref[...]
    h = _rms(x, g_ref[...]).astype(BF16)
    bcu = _dot(h, win_ref[...])
    z = bcu[:, D_MODEL:2 * D_MODEL] * bcu[:, 2 * D_MODEL:]
    r = lax.broadcasted_iota(jnp.int32, (ROW_TILE, 1), 0)
    z1 = jnp.where(r == 0, carry_ref[7:8, :], pltpu.roll(z, 1, 0))
    z2 = jnp.where(r == 0, carry_ref[6:7, :], jnp.where(r == 1, carry_ref[7:8, :], pltpu.roll(z, 2, 0)))
    y = cw_ref[0:1, :] * z2 + cw_ref[1:2, :] * z1 + cw_ref[2:3, :] * z
    carry_ref[...] = z[ROW_TILE - 8:]
    o_ref[...] = x + _dot((bcu[:, :D_MODEL] * y).astype(BF16), wout_ref[...])


def _mixer_b(x, g, w_in, conv_w, w_out):
    s = x.shape[0]
    row = pl.BlockSpec((ROW_TILE, D_MODEL), lambda i: (i, 0))
    return pl.pallas_call(
        _b_kernel,
        out_shape=jax.ShapeDtypeStruct((s, D_MODEL), F32),
        grid=(s // ROW_TILE,),
        in_specs=[row, _const_spec((1, D_MODEL)), _const_spec(w_in.shape), _const_spec(conv_w.shape),
                  _const_spec(w_out.shape)],
        out_specs=row,
        scratch_shapes=[pltpu.VMEM((8, D_MODEL), F32)],
        compiler_params=_params(("arbitrary",)),
        name="short_conv",
    )(x, g.reshape(1, -1), w_in.astype(BF16), conv_w, w_out.astype(BF16))


def _c_proj_kernel(x_ref, g_ref, wd_ref, qan_ref, kvan_ref, wq_ref, wkv_ref, qn_ref, kn_ref, cos_ref, sin_ref,
                   sign_ref, q_ref, k_ref, v_ref):
    h = _rms(x_ref[...], g_ref[...]).astype(BF16)
    d = _dot(h, wd_ref[...])
    cq = _rms(d[:, :C_Q_RANK], qan_ref[...]).astype(BF16)
    ckv = _rms(d[:, C_Q_RANK:C_Q_RANK + C_KV_RANK], kvan_ref[...]).astype(BF16)
    k_rope = pltpu.roll(d[:, C_Q_RANK + C_KV_RANK:], C_NOPE, 1)
    q_all = _dot(cq, wq_ref[...])
    kv_all = _dot(ckv, wkv_ref[...])
    first = sign_ref[...] < 0.0
    cos, sin = cos_ref[...], sin_ref[...]
    q_scale = C_DQK ** -0.5 * LOG2E

    def head(col, gain):
        ms = jnp.sum(col * col, axis=-1, keepdims=True) * (1.0 / C_DQK)
        col = col * lax.rsqrt(ms + EPS) * gain
        return _rope_apply(col, cos, sin, first, C_ROPE // 2)

    for hh in range(C_HEADS):
        cols = slice(hh * LANES, (hh + 1) * LANES)
        q_ref[hh] = (head(q_all[:, cols], qn_ref[...]) * q_scale).astype(BF16)
        k_ref[hh] = head(kv_all[:, cols] + k_rope, kn_ref[...]).astype(BF16)
    for j in range(C_HEADS // 2):
        v_ref[j] = kv_all[:, (C_HEADS + j) * LANES:(C_HEADS + j + 1) * LANES].astype(BF16)


def _c_proj(x, g, wd, qan, kvan, wq, wkv, qn, kn, cos, sin, sign):
    s = x.shape[0]
    row = lambda n: pl.BlockSpec((ROW_TILE, n), lambda i: (i, 0))
    heads = lambda n: pl.BlockSpec((n, ROW_TILE, LANES), lambda i: (0, i, 0))
    return pl.pallas_call(
        _c_proj_kernel,
        out_shape=(jax.ShapeDtypeStruct((C_HEADS, s, LANES), BF16), jax.ShapeDtypeStruct((C_HEADS, s, LANES), BF16),
                   jax.ShapeDtypeStruct((C_HEADS // 2, s, LANES), BF16)),
        grid=(s // ROW_TILE,),
        in_specs=[row(D_MODEL), _const_spec((1, D_MODEL)), _const_spec(wd.shape), _const_spec(qan.shape),
                  _const_spec(kvan.shape), _const_spec(wq.shape), _const_spec(wkv.shape), _const_spec((1, LANES)),
                  _const_spec((1, LANES)), row(LANES), row(LANES), _const_spec((1, LANES))],
        out_specs=(heads(C_HEADS), heads(C_HEADS), heads(C_HEADS // 2)),
        compiler_params=_params(("parallel",)),
        name="mla_proj",
    )(x, g, wd, qan, kvan, wq, wkv, qn, kn, cos, sin, sign)


def _c_flash_kernel(qi_ref, ki_ref, last_ref, q_ref, k_ref, v_ref, x_ref, wo_ref, o_ref, m_ref, l_ref, acc_ref,
                    s0_ref, s1_ref, p0_ref, p1_ref, r0_ref, r1_ref, a0_ref, a1_ref):
    step = pl.program_id(0)
    qi, ki = qi_ref[step], ki_ref[step]
    lo = _lane_lo()
    s_bufs, p_bufs, r_bufs, a_bufs = (s0_ref, s1_ref), (p0_ref, p1_ref), (r0_ref, r1_ref), (a0_ref, a1_ref)

    @pl.when(ki == 0)
    def _():
        m_ref[...] = jnp.full_like(m_ref, -jnp.inf)
        l_ref[...] = jnp.zeros_like(l_ref)
        acc_ref[...] = jnp.zeros_like(acc_ref)

    def sweep(masked):
        if masked:
            qpos = qi * MLA_TQ + lax.broadcasted_iota(jnp.int32, (MLA_TQ, MLA_TK), 0)
            kpos = ki * MLA_TK + lax.broadcasted_iota(jnp.int32, (MLA_TQ, MLA_TK), 1)
            causal = kpos <= qpos

        def scores(h, slot):
            s = _dot_nt(q_ref[h], k_ref[h])
            if masked:
                s = jnp.where(causal, s, -jnp.inf)
            s_bufs[slot][...] = s
            r_bufs[slot][...] = jnp.broadcast_to(jnp.max(s, axis=-1, keepdims=True), (MLA_TQ, LANES))

        def softmax(h, slot):
            m_prev = m_ref[h]
            m_new = jnp.maximum(m_prev, r_bufs[slot][...])
            alpha = jnp.exp2(m_prev - m_new)
            lsum = alpha * l_ref[h]
            for c in range(MLA_TK // LANES):
                cols = slice(c * LANES, (c + 1) * LANES)
                p = jnp.exp2(s_bufs[slot][:, cols] - m_new)
                lsum = lsum + p
                p_bufs[slot][:, cols] = p.astype(BF16)
            l_ref[h] = lsum
            m_ref[h] = m_new
            a_bufs[slot][...] = alpha

        def pv(j, slot):
            keep = lo if slot == 0 else jnp.logical_not(lo)
            acc = acc_ref[j]
            acc_ref[j] = jnp.where(keep, acc * a_bufs[slot][...] + _dot(p_bufs[slot][...], v_ref[j]), acc)

        scores(0, 0)
        scores(1, 1)
        softmax(0, 0)

        def body(i, carry):
            scores(2 * i, 0)
            softmax(2 * i - 1, 1)
            pv(i - 1, 0)
            scores(2 * i + 1, 1)
            softmax(2 * i, 0)
            pv(i - 1, 1)
            return carry

        lax.fori_loop(1, C_HEADS // 2, body, 0)
        softmax(C_HEADS - 1, 1)
        pv(C_HEADS // 2 - 1, 0)
        pv(C_HEADS // 2 - 1, 1)

    needs_mask = (ki + 1) * MLA_TK - 1 > qi * MLA_TQ

    @pl.when(needs_mask)
    def _():
        sweep(True)

    @pl.when(jnp.logical_not(needs_mask))
    def _():
        sweep(False)

    @pl.when(last_ref[step] == 1)
    def _():
        cols = []
        for j in range(C_HEADS // 2):
            l_even = jnp.sum(l_ref[2 * j], axis=-1, keepdims=True)
            l_odd = jnp.sum(l_ref[2 * j + 1], axis=-1, keepdims=True)
            cols.append((acc_ref[j] / jnp.where(lo, l_even, l_odd)).astype(BF16))
        o_ref[...] = x_ref[...] + _dot(jnp.concatenate(cols, axis=1), wo_ref[...])


def _flash_schedule(s):
    qi, ki, last = [], [], []
    for a in range(s // MLA_TQ):
        nk = -(-((a + 1) * MLA_TQ) // MLA_TK)
        for b in range(nk):
            qi.append(a)
            ki.append(b)
            last.append(int(b == nk - 1))
    return tuple(jnp.asarray(v, jnp.int32) for v in (qi, ki, last))


def _c_flash(q, k, v, x, wo):
    s = x.shape[0]
    qi, ki, last = _flash_schedule(s)
    grid_spec = pltpu.PrefetchScalarGridSpec(
        num_scalar_prefetch=3,
        grid=(int(qi.shape[0]),),
        in_specs=[
            pl.BlockSpec((C_HEADS, MLA_TQ, LANES), lambda t, qi, ki, last: (0, qi[t], 0)),
            pl.BlockSpec((C_HEADS, MLA_TK, LANES), lambda t, qi, ki, last: (0, ki[t], 0)),
            pl.BlockSpec((C_HEADS // 2, MLA_TK, LANES), lambda t, qi, ki, last: (0, ki[t], 0)),
            pl.BlockSpec((MLA_TQ, D_MODEL), lambda t, qi, ki, last: (qi[t], 0)),
            pl.BlockSpec(wo.shape, lambda t, qi, ki, last: (0, 0), pipeline_mode=pl.Buffered(1)),
        ],
        out_specs=pl.BlockSpec((MLA_TQ, D_MODEL), lambda t, qi, ki, last: (qi[t], 0)),
        scratch_shapes=[pltpu.VMEM((C_HEADS, MLA_TQ, LANES), F32), pltpu.VMEM((C_HEADS, MLA_TQ, LANES), F32),
                        pltpu.VMEM((C_HEADS // 2, MLA_TQ, LANES), F32),
                        pltpu.VMEM((MLA_TQ, MLA_TK), F32), pltpu.VMEM((MLA_TQ, MLA_TK), F32),
                        pltpu.VMEM((MLA_TQ, MLA_TK), BF16), pltpu.VMEM((MLA_TQ, MLA_TK), BF16),
                        pltpu.VMEM((MLA_TQ, LANES), F32), pltpu.VMEM((MLA_TQ, LANES), F32),
                        pltpu.VMEM((MLA_TQ, LANES), F32), pltpu.VMEM((MLA_TQ, LANES), F32)],
    )
    return pl.pallas_call(
        _c_flash_kernel,
        out_shape=jax.ShapeDtypeStruct((s, D_MODEL), F32),
        grid_spec=grid_spec,
        compiler_params=_params(("arbitrary",)),
        name="mla_flash",
    )(qi, ki, last, q, k, v, x, wo)


def _pad_lanes(v, n=LANES):
    return jnp.pad(v, (0, n - v.shape[0])).reshape(1, n)


def _mixer_c(x, g, w_down, q_a_norm, kv_a_norm, w_q_up, w_kv_up, q_norm, k_norm, w_o, tables):
    cos, sin, sign = tables
    wd = jnp.pad(w_down, ((0, 0), (0, 6 * LANES - w_down.shape[1]))).astype(BF16)
    wq = jnp.pad(w_q_up.reshape(C_Q_RANK, C_HEADS, C_DQK), ((0, 0), (0, 0), (0, LANES - C_DQK)))
    wq = wq.reshape(C_Q_RANK, C_HEADS * LANES).astype(BF16)
    wkv = w_kv_up.reshape(C_KV_RANK, C_HEADS, C_NOPE + C_V)
    wk = jnp.pad(wkv[:, :, :C_NOPE], ((0, 0), (0, 0), (0, LANES - C_NOPE))).reshape(C_KV_RANK, C_HEADS * LANES)
    wv = wkv[:, :, C_NOPE:].reshape(C_KV_RANK, C_HEADS * C_V)
    wkv = jnp.concatenate([wk, wv], axis=1).astype(BF16)
    q, k, v = _c_proj(x, g.reshape(1, -1), wd, q_a_norm.reshape(1, -1), kv_a_norm.reshape(1, -1), wq, wkv,
                      _pad_lanes(q_norm), _pad_lanes(k_norm), cos, sin, sign)
    return _c_flash(q, k, v, x, w_o.astype(BF16))


def _ffn_kernel(x_ref, g_ref, wgu_ref, wd_ref, o_ref):
    x = x_ref[...]
    h = _rms(x, g_ref[...]).astype(BF16)
    d_ff = wd_ref.shape[0]
    acc = x
    for c in range(d_ff // FFN_CHUNK):
        c0 = c * FFN_CHUNK
        gate = _dot(h, wgu_ref[:, c0:c0 + FFN_CHUNK])
        up = _dot(h, wgu_ref[:, d_ff + c0:d_ff + c0 + FFN_CHUNK])
        act = (gate * jax.nn.sigmoid(gate) * up).astype(BF16)
        acc = acc + _dot(act, wd_ref[c0:c0 + FFN_CHUNK, :])
    o_ref[...] = acc


def _ffn(x, g, w_gate_up, w_down):
    s = x.shape[0]
    row = pl.BlockSpec((ROW_TILE, D_MODEL), lambda i: (i, 0))
    return pl.pallas_call(
        _ffn_kernel,
        out_shape=jax.ShapeDtypeStruct((s, D_MODEL), F32),
        grid=(s // ROW_TILE,),
        in_specs=[row, _const_spec((1, D_MODEL)), _const_spec(w_gate_up.shape), _const_spec(w_down.shape)],
        out_specs=row,
        compiler_params=_params(("parallel",)),
        name="swiglu_ffn",
    )(x, g.reshape(1, -1), w_gate_up.astype(BF16), w_down.astype(BF16))


def kernel(x, positions, mix_norm, ffn_norm, a_w_qkv, a_q_norm, a_k_norm, a_sinks, a_w_o, b_w_in, b_conv_w, b_w_out,
           c_w_down, c_q_a_norm, c_kv_a_norm, c_w_q_up, c_w_kv_up, c_q_norm, c_k_norm, c_w_o, f_w_gate_up, f_w_down):
    batch, seq, d = x.shape
    assert batch == 1 and d == D_MODEL and seq % max(ROW_TILE, ROPE_TILE, MLA_TQ, MLA_TK) == 0
    depth = mix_norm.shape[0]
    pos_col = positions.reshape(seq, 1)
    pat_a = _rope_pattern(A_ROT_DIM, 0, A_HEAD_DIM)
    pat_c = _rope_pattern(C_ROPE, C_NOPE, LANES)
    tab_a = _rope_tables(pos_col, *pat_a) + (pat_a[1],)
    tab_c = _rope_tables(pos_col, *pat_c) + (pat_c[1],)
    xs = x.reshape(seq, d)
    for i in range(depth):
        kind, j = i % 3, i // 3
        if kind == 0:
            xs = _mixer_a(xs, mix_norm[i], a_w_qkv[j], a_q_norm[j], a_k_norm[j], a_sinks[j], a_w_o[j], tab_a)
        elif kind == 1:
            xs = _mixer_b(xs, mix_norm[i], b_w_in[j], b_conv_w[j], b_w_out[j])
        else:
            xs = _mixer_c(xs, mix_norm[i], c_w_down[j], c_q_a_norm[j], c_kv_a_norm[j], c_w_q_up[j], c_w_kv_up[j],
                          c_q_norm[j], c_k_norm[j], c_w_o[j], tab_c)
        xs = _ffn(xs, ffn_norm[i], f_w_gate_up[i], f_w_down[i])
    return xs.reshape(batch, seq, d)
```

```python
import functools
import math

import jax
import jax.numpy as jnp
import numpy as np
from jax import lax
from jax.experimental import pallas as pl
from jax.experimental.pallas import tpu as pltpu

F32 = jnp.float32
BF16 = jnp.bfloat16

D_MODEL = 1024
ROPE_THETA = 500000.0
EPS = 1e-6
A_HEADS, A_KV_HEADS, A_HEAD_DIM, A_ROT_DIM, A_WINDOW = 16, 4, 64, 16, 128
C_HEADS, C_NOPE, C_ROPE, C_V, C_Q_RANK, C_KV_RANK = 16, 64, 32, 64, 384, 256
C_DQK = C_NOPE + C_ROPE
LOG2E = math.log2(math.e)

LANES = 128
VMEM_LIMIT_BYTES = 56 * 2**20

ROW_TILE = 512
ROPE_TILE = 2048
FFN_CHUNK = 256
MLA_TQ = 512
MLA_TK = 512
SOFTMAX_ROWS = 64


def _params(sem, flags=None):
    return pltpu.CompilerParams(dimension_semantics=sem, vmem_limit_bytes=VMEM_LIMIT_BYTES, flags=flags)


def _const_spec(shape):
    return pl.BlockSpec(shape, lambda *_: (0,) * len(shape), pipeline_mode=pl.Buffered(1))


def _rms(x, g):
    ms = jnp.mean(x * x, axis=-1, keepdims=True)
    return x * lax.rsqrt(ms + EPS) * g


def _dot(a, b):
    return jnp.dot(a, b, preferred_element_type=F32)


def _dot_nt(a, b):
    return lax.dot_general(a, b, (((1,), (1,)), ((), ())), preferred_element_type=F32)


def _lane_lo():
    return lax.broadcasted_iota(jnp.int32, (1, LANES), 1) < (LANES // 2)


def _rope_apply(x, cos, sin, first_half, half):
    fwd = pltpu.roll(x, LANES - half, 1)
    bwd = pltpu.roll(x, half, 1)
    return x * cos + jnp.where(first_half, fwd, bwd) * sin


def _rope_kernel(pos_ref, invf_ref, sign_ref, cos_ref, sin_ref):
    ang = pos_ref[...].astype(F32) * invf_ref[...]
    cos_ref[...] = jnp.cos(ang)
    sin_ref[...] = jnp.sin(ang) * sign_ref[...]


def _rope_tables(pos_col, invf, sign):
    s = pos_col.shape[0]
    row = pl.BlockSpec((1, LANES), lambda i: (0, 0))
    out = pl.BlockSpec((ROPE_TILE, LANES), lambda i: (i, 0))
    return pl.pallas_call(
        _rope_kernel,
        out_shape=(jax.ShapeDtypeStruct((s, LANES), F32),) * 2,
        grid=(s // ROPE_TILE,),
        in_specs=[pl.BlockSpec((ROPE_TILE, 1), lambda i: (i, 0)), row, row],
        out_specs=(out, out),
        compiler_params=_params(("parallel",)),
        name="rope_tables",
    )(pos_col, invf, sign)


def _rope_pattern(rot_dim, offset, period):
    half = rot_dim // 2
    inv_freq = ROPE_THETA ** (-jnp.arange(0, rot_dim, 2, dtype=F32) / rot_dim)
    d = np.arange(LANES) % period - offset
    rotary = (d >= 0) & (d < rot_dim)
    idx = np.where(rotary, d % half, 0)
    invf = jnp.where(jnp.asarray(rotary), inv_freq[idx], 0.0)
    sign = np.where(rotary, np.where(d < half, -1.0, 1.0), 0.0)
    return invf.reshape(1, LANES).astype(F32), jnp.asarray(sign, F32).reshape(1, LANES)


def _a_proj_kernel(x_ref, g_ref, w_ref, qn_ref, kn_ref, cos_ref, sin_ref, sign_ref, q_ref, k_ref, v_ref):
    h = _rms(x_ref[...], g_ref[...]).astype(BF16)
    qkv = _dot(h, w_ref[...])
    lo = _lane_lo()
    first = sign_ref[...] < 0.0
    cos, sin = cos_ref[...], sin_ref[...]
    nq = A_HEADS * A_HEAD_DIM
    nk = 2 * A_KV_HEADS * LANES
    q_scale = A_HEAD_DIM ** -0.5 * LOG2E
    for c in range(nq // LANES):
        col = qkv[:, c * LANES:(c + 1) * LANES]
        sq = col * col
        s_lo = jnp.sum(jnp.where(lo, sq, 0.0), axis=-1, keepdims=True)
        s_hi = jnp.sum(jnp.where(lo, 0.0, sq), axis=-1, keepdims=True)
        ms = jnp.where(lo, s_lo, s_hi) * (1.0 / A_HEAD_DIM)
        col = col * lax.rsqrt(ms + EPS) * qn_ref[...]
        col = _rope_apply(col, cos, sin, first, A_ROT_DIM // 2) * q_scale
        q_ref[:, c * LANES:(c + 1) * LANES] = col.astype(BF16)
    for c in range(2 * A_KV_HEADS):
        col = qkv[:, nq + c * LANES:nq + (c + 1) * LANES]
        ms = jnp.sum(col * col, axis=-1, keepdims=True) * (1.0 / A_HEAD_DIM)
        col = col * lax.rsqrt(ms + EPS) * kn_ref[...]
        col = _rope_apply(col, cos, sin, first, A_ROT_DIM // 2)
        k_ref[:, c * LANES:(c + 1) * LANES] = col.astype(BF16)
    v_ref[...] = qkv[:, nq + nk:].astype(BF16)


def _a_proj(x, g, w, qn, kn, cos, sin, sign):
    s = x.shape[0]
    nq, nk, nv = A_HEADS * A_HEAD_DIM, 2 * A_KV_HEADS * LANES, A_KV_HEADS * LANES
    row = lambda n: pl.BlockSpec((ROW_TILE, n), lambda i: (i, 0))
    return pl.pallas_call(
        _a_proj_kernel,
        out_shape=(jax.ShapeDtypeStruct((s, nq), BF16), jax.ShapeDtypeStruct((s, nk), BF16),
                   jax.ShapeDtypeStruct((s, nv), BF16)),
        grid=(s // ROW_TILE,),
        in_specs=[row(D_MODEL), _const_spec((1, D_MODEL)), _const_spec(w.shape), _const_spec((1, LANES)),
                  _const_spec((1, LANES)), row(LANES), row(LANES), _const_spec((1, LANES))],
        out_specs=(row(nq), row(nk), row(nv)),
        compiler_params=_params(("parallel",)),
        name="swa_qkv_proj",
    )(x, g, w, qn, kn, cos, sin, sign)


def _a_attn_kernel(sink_ref, q_ref, kp_ref, kc_ref, vp_ref, vc_ref, x_ref, wo_ref, o_ref, oacc_ref):
    i = pl.program_id(0)
    blk = A_WINDOW
    kcat = jnp.concatenate([kp_ref[...], kc_ref[...]], axis=0)
    vcat = jnp.concatenate([vp_ref[...], vc_ref[...]], axis=0)
    lo = _lane_lo()
    qi = lax.broadcasted_iota(jnp.int32, (2 * blk, 2 * blk), 0) & (blk - 1)
    kj = lax.broadcasted_iota(jnp.int32, (2 * blk, 2 * blk), 1)
    first_key = jnp.where(i > 0, 0, blk)
    for b in range(ROW_TILE // blk):
        valid = (kj > qi) & (kj <= qi + blk)
        if b == 0:
            valid = valid & (kj >= first_key)
        rows = slice(b * blk, (b + 1) * blk)
        for g in range(A_KV_HEADS):
            qs = jnp.concatenate([q_ref[rows, (2 * g) * LANES:(2 * g + 1) * LANES],
                                  q_ref[rows, (2 * g + 1) * LANES:(2 * g + 2) * LANES]], axis=0)
            vd = vcat[b * blk:(b + 2) * blk, g * LANES:(g + 1) * LANES]
            outs = []
            for half in range(2):
                kd = kcat[b * blk:(b + 2) * blk, (2 * g + half) * LANES:(2 * g + half + 1) * LANES]
                sink = jnp.concatenate([jnp.full((blk, 1), sink_ref[4 * g + half] * LOG2E, F32),
                                        jnp.full((blk, 1), sink_ref[4 * g + 2 + half] * LOG2E, F32)], axis=0)
                s = jnp.where(valid, _dot_nt(qs, kd), -jnp.inf)
                m = jnp.maximum(jnp.max(s, axis=-1, keepdims=True), sink)
                p = jnp.exp2(s - m)
                denom = jnp.sum(p, axis=-1, keepdims=True) + jnp.exp2(sink - m)
                outs.append(_dot(p.astype(BF16), vd) / denom)
            pair = jnp.where(lo, outs[0], outs[1])
            oacc_ref[rows, (2 * g) * LANES:(2 * g + 1) * LANES] = pair[:blk].astype(BF16)
            oacc_ref[rows, (2 * g + 1) * LANES:(2 * g + 2) * LANES] = pair[blk:].astype(BF16)
    o_ref[...] = x_ref[...] + _dot(oacc_ref[...], wo_ref[...])


def _a_attn(sinks, q, k, v, x, wo):
    s = x.shape[0]
    nk, nv = 2 * A_KV_HEADS * LANES, A_KV_HEADS * LANES
    per = ROW_TILE // A_WINDOW
    row = lambda n: pl.BlockSpec((ROW_TILE, n), lambda i: (i, 0))
    prev = lambda n: pl.BlockSpec((A_WINDOW, n), lambda i: (jnp.maximum(i * per - 1, 0), 0))
    return pl.pallas_call(
        _a_attn_kernel,
        out_shape=jax.ShapeDtypeStruct((s, D_MODEL), F32),
        grid=(s // ROW_TILE,),
        in_specs=[pl.BlockSpec(memory_space=pltpu.SMEM), row(A_HEADS * A_HEAD_DIM), prev(nk), row(nk), prev(nv),
                  row(nv), row(D_MODEL), _const_spec(wo.shape)],
        out_specs=row(D_MODEL),
        scratch_shapes=[pltpu.VMEM((ROW_TILE, A_HEADS * A_HEAD_DIM), BF16)],
        compiler_params=_params(("parallel",)),
        name="swa_attention",
    )(sinks, q, k, k, v, v, x, wo)


def _mixer_a(x, g, w_qkv, q_norm, k_norm, sinks, w_o, tables):
    cos, sin, sign = tables
    nq = A_HEADS * A_HEAD_DIM
    nkv = A_KV_HEADS * A_HEAD_DIM
    wq = w_qkv[:, :nq]
    wk = w_qkv[:, nq:nq + nkv].reshape(D_MODEL, A_KV_HEADS, 1, A_HEAD_DIM)
    wv = w_qkv[:, nq + nkv:].reshape(D_MODEL, A_KV_HEADS, 1, A_HEAD_DIM)
    zero = jnp.zeros_like(wk)
    wk = jnp.concatenate([wk, zero, zero, wk], axis=2).reshape(D_MODEL, 2 * A_KV_HEADS * LANES)
    wv = jnp.concatenate([wv, wv], axis=2).reshape(D_MODEL, A_KV_HEADS * LANES)
    w = jnp.concatenate([wq, wk, wv], axis=1).astype(BF16)
    qn = jnp.tile(q_norm, 2).reshape(1, LANES)
    kn = jnp.tile(k_norm, 2).reshape(1, LANES)
    q, k, v = _a_proj(x, g.reshape(1, -1), w, qn, kn, cos, sin, sign)
    return _a_attn(sinks, q, k, v, x, w_o.astype(BF16))


def _b_kernel(x_ref, g_ref, win_ref, cw_ref, wout_ref, o_ref, carry_ref):
    @pl.when(pl.program_id(0) == 0)
    def _():
        carry_ref[...] = jnp.zeros_like(carry_ref)

    x = x_ref[...]
    h = _rms(x, g_ref[...]).astype(BF16)
    bcu = _dot(h, win_ref[...])
    z = bcu[:, D_MODEL:2 * D_MODEL] * bcu[:, 2 * D_MODEL:]
    r = lax.broadcasted_iota(jnp.int32, (ROW_TILE, 1), 0)
    z1 = jnp.where(r == 0, carry_ref[7:8, :], pltpu.roll(z, 1, 0))
    z2 = jnp.where(r == 0, carry_ref[6:7, :], jnp.where(r == 1, carry_ref[7:8, :], pltpu.roll(z, 2, 0)))
    y = cw_ref[0:1, :] * z2 + cw_ref[1:2, :] * z1 + cw_ref[2:3, :] * z
    carry_ref[...] = z[ROW_TILE - 8:]
    o_ref[...] = x + _dot((bcu[:, :D_MODEL] * y).astype(BF16), wout_ref[...])


def _mixer_b(x, g, w_in, conv_w, w_out):
    s = x.shape[0]
    row = pl.BlockSpec((ROW_TILE, D_MODEL), lambda i: (i, 0))
    return pl.pallas_call(
        _b_kernel,
        out_shape=jax.ShapeDtypeStruct((s, D_MODEL), F32),
        grid=(s // ROW_TILE,),
        in_specs=[row, _const_spec((1, D_MODEL)), _const_spec(w_in.shape), _const_spec(conv_w.shape),
                  _const_spec(w_out.shape)],
        out_specs=row,
        scratch_shapes=[pltpu.VMEM((8, D_MODEL), F32)],
        compiler_params=_params(("arbitrary",)),
        name="short_conv",
    )(x, g.reshape(1, -1), w_in.astype(BF16), conv_w, w_out.astype(BF16))


def _c_proj_kernel(x_ref, g_ref, wd_ref, qan_ref, kvan_ref, wq_ref, wkv_ref, qn_ref, kn_ref, cos_ref, sin_ref,
                   sign_ref, q_ref, k_ref, v_ref):
    h = _rms(x_ref[...], g_ref[...]).astype(BF16)
    d = _dot(h, wd_ref[...])
    cq = _rms(d[:, :C_Q_RANK], qan_ref[...]).astype(BF16)
    ckv = _rms(d[:, C_Q_RANK:C_Q_RANK + C_KV_RANK], kvan_ref[...]).astype(BF16)
    k_rope = pltpu.roll(d[:, C_Q_RANK + C_KV_RANK:], C_NOPE, 1)
    q_all = _dot(cq, wq_ref[...])
    kv_all = _dot(ckv, wkv_ref[...])
    first = sign_ref[...] < 0.0
    cos, sin = cos_ref[...], sin_ref[...]
    q_scale = C_DQK ** -0.5 * LOG2E

    def head(col, gain):
        ms = jnp.sum(col * col, axis=-1, keepdims=True) * (1.0 / C_DQK)
        col = col * lax.rsqrt(ms + EPS) * gain
        return _rope_apply(col, cos, sin, first, C_ROPE // 2)

    for hh in range(C_HEADS):
        cols = slice(hh * LANES, (hh + 1) * LANES)
        q_ref[hh] = (head(q_all[:, cols], qn_ref[...]) * q_scale).astype(BF16)
        k_ref[hh] = head(kv_all[:, cols] + k_rope, kn_ref[...]).astype(BF16)
    for j in range(C_HEADS // 2):
        v_ref[j] = kv_all[:, (C_HEADS + j) * LANES:(C_HEADS + j + 1) * LANES].astype(BF16)


def _c_proj(x, g, wd, qan, kvan, wq, wkv, qn, kn, cos, sin, sign):
    s = x.shape[0]
    row = lambda n: pl.BlockSpec((ROW_TILE, n), lambda i: (i, 0))
    heads = lambda n: pl.BlockSpec((n, ROW_TILE, LANES), lambda i: (0, i, 0))
    return pl.pallas_call(
        _c_proj_kernel,
        out_shape=(jax.ShapeDtypeStruct((C_HEADS, s, LANES), BF16), jax.ShapeDtypeStruct((C_HEADS, s, LANES), BF16),
                   jax.ShapeDtypeStruct((C_HEADS // 2, s, LANES), BF16)),
        grid=(s // ROW_TILE,),
        in_specs=[row(D_MODEL), _const_spec((1, D_MODEL)), _const_spec(wd.shape), _const_spec(qan.shape),
                  _const_spec(kvan.shape), _const_spec(wq.shape), _const_spec(wkv.shape), _const_spec((1, LANES)),
                  _const_spec((1, LANES)), row(LANES), row(LANES), _const_spec((1, LANES))],
        out_specs=(heads(C_HEADS), heads(C_HEADS), heads(C_HEADS // 2)),
        compiler_params=_params(("parallel",)),
        name="mla_proj",
    )(x, g, wd, qan, kvan, wq, wkv, qn, kn, cos, sin, sign)


def _c_flash_kernel(qi_ref, ki_ref, last_ref, q_ref, k_ref, v_ref, x_ref, wo_ref, o_ref, m_ref, l_ref, acc_ref,
                    s0_ref, s1_ref, p0_ref, p1_ref, a0_ref, a1_ref):
    step = pl.program_id(0)
    qi, ki = qi_ref[step], ki_ref[step]
    lo = _lane_lo()
    s_bufs, p_bufs, a_bufs = (s0_ref, s1_ref), (p0_ref, p1_ref), (a0_ref, a1_ref)

    @pl.when(ki == 0)
    def _():
        m_ref[...] = jnp.full_like(m_ref, -jnp.inf)
        l_ref[...] = jnp.zeros_like(l_ref)
        acc_ref[...] = jnp.zeros_like(acc_ref)

    def sweep(masked):
        if masked:
            qpos = qi * MLA_TQ + lax.broadcasted_iota(jnp.int32, (MLA_TQ, MLA_TK), 0)
            kpos = ki * MLA_TK + lax.broadcasted_iota(jnp.int32, (MLA_TQ, MLA_TK), 1)
            causal = kpos <= qpos

        def scores(h, slot):
            s = _dot_nt(q_ref[h], k_ref[h])
            if masked:
                s = jnp.where(causal, s, -jnp.inf)
            s_bufs[slot][...] = s

        def softmax(h, slot):
            s_ref, p_ref = s_bufs[slot], p_bufs[slot]
            for rb in range(MLA_TQ // SOFTMAX_ROWS):
                rows = slice(rb * SOFTMAX_ROWS, (rb + 1) * SOFTMAX_ROWS)
                chunks = [slice(c * LANES, (c + 1) * LANES) for c in range(MLA_TK // LANES)]
                part = s_ref[rows, chunks[0]]
                for cols in chunks[1:]:
                    part = jnp.maximum(part, s_ref[rows, cols])
                m_prev = m_ref[h, rows, :]
                m_new = jnp.maximum(m_prev, jnp.max(part, axis=-1, keepdims=True))
                alpha = jnp.exp2(m_prev - m_new)
                lsum = alpha * l_ref[h, rows, :]
                for cols in chunks:
                    p = jnp.exp2(s_ref[rows, cols] - m_new)
                    lsum = lsum + p
                    p_ref[rows, cols] = p.astype(BF16)
                l_ref[h, rows, :] = lsum
                m_ref[h, rows, :] = m_new
                a_bufs[slot][rows, :] = alpha

        def pv(j, slot):
            keep = lo if slot == 0 else jnp.logical_not(lo)
            acc = acc_ref[j]
            acc_ref[j] = jnp.where(keep, acc * a_bufs[slot][...] + _dot(p_bufs[slot][...], v_ref[j]), acc)

        for t in range(C_HEADS + 2):
            if t < C_HEADS:
                scores(t, t % 2)
            if 1 <= t <= C_HEADS:
                softmax(t - 1, (t - 1) % 2)
            if t >= 2:
                pv((t - 2) // 2, t % 2)

    needs_mask = (ki + 1) * MLA_TK - 1 > qi * MLA_TQ

    @pl.when(needs_mask)
    def _():
        sweep(True)

    @pl.when(jnp.logical_not(needs_mask))
    def _():
        sweep(False)

    @pl.when(last_ref[step] == 1)
    def _():
        cols = []
        for j in range(C_HEADS // 2):
            l_even = jnp.sum(l_ref[2 * j], axis=-1, keepdims=True)
            l_odd = jnp.sum(l_ref[2 * j + 1], axis=-1, keepdims=True)
            cols.append((acc_ref[j] / jnp.where(lo, l_even, l_odd)).astype(BF16))
        o_ref[...] = x_ref[...] + _dot(jnp.concatenate(cols, axis=1), wo_ref[...])


def _flash_schedule(s):
    qi, ki, last = [], [], []
    for a in range(s // MLA_TQ):
        nk = -(-((a + 1) * MLA_TQ) // MLA_TK)
        for b in range(nk):
            qi.append(a)
            ki.append(b)
            last.append(int(b == nk - 1))
    return tuple(jnp.asarray(v, jnp.int32) for v in (qi, ki, last))


def _c_flash(q, k, v, x, wo):
    s = x.shape[0]
    qi, ki, last = _flash_schedule(s)
    grid_spec = pltpu.PrefetchScalarGridSpec(
        num_scalar_prefetch=3,
        grid=(int(qi.shape[0]),),
        in_specs=[
            pl.BlockSpec((C_HEADS, MLA_TQ, LANES), lambda t, qi, ki, last: (0, qi[t], 0)),
            pl.BlockSpec((C_HEADS, MLA_TK, LANES), lambda t, qi, ki, last: (0, ki[t], 0)),
            pl.BlockSpec((C_HEADS // 2, MLA_TK, LANES), lambda t, qi, ki, last: (0, ki[t], 0)),
            pl.BlockSpec((MLA_TQ, D_MODEL), lambda t, qi, ki, last: (qi[t], 0)),
            pl.BlockSpec(wo.shape, lambda t, qi, ki, last: (0, 0), pipeline_mode=pl.Buffered(1)),
        ],
        out_specs=pl.BlockSpec((MLA_TQ, D_MODEL), lambda t, qi, ki, last: (qi[t], 0)),
        scratch_shapes=[pltpu.VMEM((C_HEADS, MLA_TQ, LANES), F32), pltpu.VMEM((C_HEADS, MLA_TQ, LANES), F32),
                        pltpu.VMEM((C_HEADS // 2, MLA_TQ, LANES), F32),
                        pltpu.VMEM((MLA_TQ, MLA_TK), F32), pltpu.VMEM((MLA_TQ, MLA_TK), F32),
                        pltpu.VMEM((MLA_TQ, MLA_TK), BF16), pltpu.VMEM((MLA_TQ, MLA_TK), BF16),
                        pltpu.VMEM((MLA_TQ, LANES), F32), pltpu.VMEM((MLA_TQ, LANES), F32)],
    )
    return pl.pallas_call(
        _c_flash_kernel,
        out_shape=jax.ShapeDtypeStruct((s, D_MODEL), F32),
        grid_spec=grid_spec,
        compiler_params=_params(("arbitrary",)),
        name="mla_flash",
    )(qi, ki, last, q, k, v, x, wo)


def _pad_lanes(v, n=LANES):
    return jnp.pad(v, (0, n - v.shape[0])).reshape(1, n)


def _mixer_c(x, g, w_down, q_a_norm, kv_a_norm, w_q_up, w_kv_up, q_norm, k_norm, w_o, tables):
    cos, sin, sign = tables
    wd = jnp.pad(w_down, ((0, 0), (0, 6 * LANES - w_down.shape[1]))).astype(BF16)
    wq = jnp.pad(w_q_up.reshape(C_Q_RANK, C_HEADS, C_DQK), ((0, 0), (0, 0), (0, LANES - C_DQK)))
    wq = wq.reshape(C_Q_RANK, C_HEADS * LANES).astype(BF16)
    wkv = w_kv_up.reshape(C_KV_RANK, C_HEADS, C_NOPE + C_V)
    wk = jnp.pad(wkv[:, :, :C_NOPE], ((0, 0), (0, 0), (0, LANES - C_NOPE))).reshape(C_KV_RANK, C_HEADS * LANES)
    wv = wkv[:, :, C_NOPE:].reshape(C_KV_RANK, C_HEADS * C_V)
    wkv = jnp.concatenate([wk, wv], axis=1).astype(BF16)
    q, k, v = _c_proj(x, g.reshape(1, -1), wd, q_a_norm.reshape(1, -1), kv_a_norm.reshape(1, -1), wq, wkv,
                      _pad_lanes(q_norm), _pad_lanes(k_norm), cos, sin, sign)
    return _c_flash(q, k, v, x, w_o.astype(BF16))


def _ffn_kernel(x_ref, g_ref, wgu_ref, wd_ref, o_ref):
    x = x_ref[...]
    h = _rms(x, g_ref[...]).astype(BF16)
    d_ff = wd_ref.shape[0]
    acc = x
    for c in range(d_ff // FFN_CHUNK):
        c0 = c * FFN_CHUNK
        gate = _dot(h, wgu_ref[:, c0:c0 + FFN_CHUNK])
        up = _dot(h, wgu_ref[:, d_ff + c0:d_ff + c0 + FFN_CHUNK])
        act = (gate * jax.nn.sigmoid(gate) * up).astype(BF16)
        acc = acc + _dot(act, wd_ref[c0:c0 + FFN_CHUNK, :])
    o_ref[...] = acc


def _ffn(x, g, w_gate_up, w_down):
    s = x.shape[0]
    row = pl.BlockSpec((ROW_TILE, D_MODEL), lambda i: (i, 0))
    return pl.pallas_call(
        _ffn_kernel,
        out_shape=jax.ShapeDtypeStruct((s, D_MODEL), F32),
        grid=(s // ROW_TILE,),
        in_specs=[row, _const_spec((1, D_MODEL)), _const_spec(w_gate_up.shape), _const_spec(w_down.shape)],
        out_specs=row,
        compiler_params=_params(("parallel",)),
        name="swiglu_ffn",
    )(x, g.reshape(1, -1), w_gate_up.astype(BF16), w_down.astype(BF16))


def kernel(x, positions, mix_norm, ffn_norm, a_w_qkv, a_q_norm, a_k_norm, a_sinks, a_w_o, b_w_in, b_conv_w, b_w_out,
           c_w_down, c_q_a_norm, c_kv_a_norm, c_w_q_up, c_w_kv_up, c_q_norm, c_k_norm, c_w_o, f_w_gate_up, f_w_down):
    batch, seq, d = x.shape
    assert batch == 1 and d == D_MODEL and seq % max(ROW_TILE, ROPE_TILE, MLA_TQ, MLA_TK) == 0
    depth = mix_norm.shape[0]
    pos_col = positions.reshape(seq, 1)
    pat_a = _rope_pattern(A_ROT_DIM, 0, A_HEAD_DIM)
    pat_c = _rope_pattern(C_ROPE, C_NOPE, LANES)
    tab_a = _rope_tables(pos_col, *pat_a) + (pat_a[1],)
    tab_c = _rope_tables(pos_col, *pat_c) + (pat_c[1],)
    xs = x.reshape(seq, d)
    for i in range(depth):
        kind, j = i % 3, i // 3
        if kind == 0:
            xs = _mixer_a(xs, mix_norm[i], a_w_qkv[j], a_q_norm[j], a_k_norm[j], a_sinks[j], a_w_o[j], tab_a)
        elif kind == 1:
            xs = _mixer_b(xs, mix_norm[i], b_w_in[j], b_conv_w[j], b_w_out[j])
        else:
            xs = _mixer_c(xs, mix_norm[i], c_w_down[j], c_q_a_norm[j], c_kv_a_norm[j], c_w_q_up[j], c_w_kv_up[j],
                          c_q_norm[j], c_k_norm[j], c_w_o[j], tab_c)
        xs = _ffn(xs, ffn_norm[i], f_w_gate_up[i], f_w_down[i])
    return xs.reshape(batch, seq, d)
```

```python
import functools
import math

import jax
import jax.numpy as jnp
import numpy as np
from jax import lax
from jax.experimental import pallas as pl
from jax.experimental.pallas import tpu as pltpu

F32 = jnp.float32
BF16 = jnp.bfloat16

D_MODEL = 1024
ROPE_THETA = 500000.0
EPS = 1e-6
A_HEADS, A_KV_HEADS, A_HEAD_DIM, A_ROT_DIM, A_WINDOW = 16, 4, 64, 16, 128
C_HEADS, C_NOPE, C_ROPE, C_V, C_Q_RANK, C_KV_RANK = 16, 64, 32, 64, 384, 256
C_DQK = C_NOPE + C_ROPE
LOG2E = math.log2(math.e)

LANES = 128
VMEM_LIMIT_BYTES = 56 * 2**20

ROW_TILE = 512
ROPE_TILE = 2048
FFN_CHUNK = 256
MLA_TQ = 512
MLA_TK = 512
SOFTMAX_ROWS = 64


def _params(sem, flags=None):
    return pltpu.CompilerParams(dimension_semantics=sem, vmem_limit_bytes=VMEM_LIMIT_BYTES, flags=flags)


def _const_spec(shape):
    return pl.BlockSpec(shape, lambda *_: (0,) * len(shape), pipeline_mode=pl.Buffered(1))


def _rms(x, g):
    ms = jnp.mean(x * x, axis=-1, keepdims=True)
    return x * lax.rsqrt(ms + EPS) * g


def _dot(a, b):
    return jnp.dot(a, b, preferred_element_type=F32)


def _dot_nt(a, b):
    return lax.dot_general(a, b, (((1,), (1,)), ((), ())), preferred_element_type=F32)


def _lane_lo():
    return lax.broadcasted_iota(jnp.int32, (1, LANES), 1) < (LANES // 2)


def _rope_apply(x, cos, sin, first_half, half):
    fwd = pltpu.roll(x, LANES - half, 1)
    bwd = pltpu.roll(x, half, 1)
    return x * cos + jnp.where(first_half, fwd, bwd) * sin


def _rope_kernel(pos_ref, invf_ref, sign_ref, cos_ref, sin_ref):
    ang = pos_ref[...].astype(F32) * invf_ref[...]
    cos_ref[...] = jnp.cos(ang)
    sin_ref[...] = jnp.sin(ang) * sign_ref[...]


def _rope_tables(pos_col, invf, sign):
    s = pos_col.shape[0]
    row = pl.BlockSpec((1, LANES), lambda i: (0, 0))
    out = pl.BlockSpec((ROPE_TILE, LANES), lambda i: (i, 0))
    return pl.pallas_call(
        _rope_kernel,
        out_shape=(jax.ShapeDtypeStruct((s, LANES), F32),) * 2,
        grid=(s // ROPE_TILE,),
        in_specs=[pl.BlockSpec((ROPE_TILE, 1), lambda i: (i, 0)), row, row],
        out_specs=(out, out),
        compiler_params=_params(("parallel",)),
        name="rope_tables",
    )(pos_col, invf, sign)


def _rope_pattern(rot_dim, offset, period):
    half = rot_dim // 2
    inv_freq = ROPE_THETA ** (-jnp.arange(0, rot_dim, 2, dtype=F32) / rot_dim)
    d = np.arange(LANES) % period - offset
    rotary = (d >= 0) & (d < rot_dim)
    idx = np.where(rotary, d % half, 0)
    invf = jnp.where(jnp.asarray(rotary), inv_freq[idx], 0.0)
    sign = np.where(rotary, np.where(d < half, -1.0, 1.0), 0.0)
    return invf.reshape(1, LANES).astype(F32), jnp.asarray(sign, F32).reshape(1, LANES)


def _a_proj_kernel(x_ref, g_ref, w_ref, gain_ref, ones_ref, cos_ref, sin_ref, sign_ref, q_ref, k_ref, v_ref):
    h = _rms(x_ref[...], g_ref[...]).astype(BF16)
    qkv = _dot(h, w_ref[...])
    lo = _lane_lo()
    first = sign_ref[...] < 0.0
    cos, sin = cos_ref[...], sin_ref[...]
    nq = A_HEADS * A_HEAD_DIM
    nkv = A_KV_HEADS * A_HEAD_DIM
    q_scale = A_HEAD_DIM ** -0.5 * LOG2E
    half_lanes = LANES // 2

    def split(col):
        a0 = jnp.where(lo, col, 0.0)
        b1 = jnp.where(lo, 0.0, col)
        return a0, pltpu.roll(a0, half_lanes, 1), pltpu.roll(b1, half_lanes, 1), b1

    for pr in range((nq + nkv) // (2 * LANES)):
        blk = qkv[:, pr * 2 * LANES:(pr + 1) * 2 * LANES]
        ss = _dot((blk * blk).astype(BF16), ones_ref[...])
        y = blk * lax.rsqrt(ss * (1.0 / A_HEAD_DIM) + EPS) * gain_ref[:, pr * 2 * LANES:(pr + 1) * 2 * LANES]
        for c in range(2):
            col = _rope_apply(y[:, c * LANES:(c + 1) * LANES], cos, sin, first, A_ROT_DIM // 2)
            cidx = 2 * pr + c
            if cidx < nq // LANES:
                q_ref[:, cidx * LANES:(cidx + 1) * LANES] = (col * q_scale).astype(BF16)
            else:
                kc = cidx - nq // LANES
                for t, part in enumerate(split(col)):
                    k_ref[:, (4 * kc + t) * LANES:(4 * kc + t + 1) * LANES] = part.astype(BF16)
    for c in range(nkv // LANES):
        a0, a1, b0, b1 = split(qkv[:, nq + nkv + c * LANES:nq + nkv + (c + 1) * LANES])
        v_ref[:, (2 * c) * LANES:(2 * c + 1) * LANES] = (a0 + a1).astype(BF16)
        v_ref[:, (2 * c + 1) * LANES:(2 * c + 2) * LANES] = (b0 + b1).astype(BF16)


def _a_proj(x, g, w, gain, ones, cos, sin, sign):
    s = x.shape[0]
    nq, nk, nv = A_HEADS * A_HEAD_DIM, 2 * A_KV_HEADS * LANES, A_KV_HEADS * LANES
    row = lambda n: pl.BlockSpec((ROW_TILE, n), lambda i: (i, 0))
    return pl.pallas_call(
        _a_proj_kernel,
        out_shape=(jax.ShapeDtypeStruct((s, nq), BF16), jax.ShapeDtypeStruct((s, nk), BF16),
                   jax.ShapeDtypeStruct((s, nv), BF16)),
        grid=(s // ROW_TILE,),
        in_specs=[row(D_MODEL), _const_spec((1, D_MODEL)), _const_spec(w.shape), _const_spec(gain.shape),
                  _const_spec(ones.shape), row(LANES), row(LANES), _const_spec((1, LANES))],
        out_specs=(row(nq), row(nk), row(nv)),
        compiler_params=_params(("parallel",)),
        name="swa_qkv_proj",
    )(x, g, w, gain, ones, cos, sin, sign)


def _block_ones(block, valid):
    r = np.arange(2 * LANES)
    same = (r[:, None] // block) == (r[None, :] // block)
    return jnp.asarray(same & ((r % block) < valid)[:, None], BF16)


def _a_attn_kernel(sink_ref, q_ref, kp_ref, kc_ref, vp_ref, vc_ref, x_ref, wo_ref, o_ref, oacc_ref):
    i = pl.program_id(0)
    blk = A_WINDOW
    kcat = jnp.concatenate([kp_ref[...], kc_ref[...]], axis=0)
    vcat = jnp.concatenate([vp_ref[...], vc_ref[...]], axis=0)
    lo = _lane_lo()
    qi = lax.broadcasted_iota(jnp.int32, (2 * blk, 2 * blk), 0) & (blk - 1)
    kj = lax.broadcasted_iota(jnp.int32, (2 * blk, 2 * blk), 1)
    first_key = jnp.where(i > 0, 0, blk)
    ones = jnp.ones((2 * blk, LANES), BF16)
    for b in range(ROW_TILE // blk):
        valid = (kj > qi) & (kj <= qi + blk)
        if b == 0:
            valid = valid & (kj >= first_key)
        rows = slice(b * blk, (b + 1) * blk)
        for g in range(A_KV_HEADS):
            qs = jnp.concatenate([q_ref[rows, (2 * g) * LANES:(2 * g + 1) * LANES],
                                  q_ref[rows, (2 * g + 1) * LANES:(2 * g + 2) * LANES]], axis=0)
            vd = jnp.concatenate([vcat[b * blk:(b + 2) * blk, g * LANES:(g + 1) * LANES], ones], axis=1)
            outs = []
            for half in range(2):
                kd = kcat[b * blk:(b + 2) * blk, (2 * g + half) * LANES:(2 * g + half + 1) * LANES]
                sink = jnp.concatenate([jnp.full((blk, LANES), sink_ref[4 * g + half] * LOG2E, F32),
                                        jnp.full((blk, LANES), sink_ref[4 * g + 2 + half] * LOG2E, F32)], axis=0)
                s = jnp.where(valid, _dot_nt(qs, kd), -jnp.inf)
                s0, s1 = s[:, :LANES], s[:, LANES:]
                m = jnp.maximum(jnp.max(jnp.maximum(s0, s1), axis=-1, keepdims=True), sink)
                p = jnp.concatenate([jnp.exp2(s0 - m), jnp.exp2(s1 - m)], axis=1).astype(BF16)
                ol = _dot(p, vd)
                outs.append(ol[:, :LANES] / (ol[:, LANES:] + jnp.exp2(sink - m)))
            pair = jnp.where(lo, outs[0], outs[1])
            oacc_ref[rows, (2 * g) * LANES:(2 * g + 1) * LANES] = pair[:blk].astype(BF16)
            oacc_ref[rows, (2 * g + 1) * LANES:(2 * g + 2) * LANES] = pair[blk:].astype(BF16)
    o_ref[...] = x_ref[...] + _dot(oacc_ref[...], wo_ref[...])


def _a_attn(sinks, q, k, v, x, wo):
    s = x.shape[0]
    nk, nv = 2 * A_KV_HEADS * LANES, A_KV_HEADS * LANES
    per = ROW_TILE // A_WINDOW
    row = lambda n: pl.BlockSpec((ROW_TILE, n), lambda i: (i, 0))
    prev = lambda n: pl.BlockSpec((A_WINDOW, n), lambda i: (jnp.maximum(i * per - 1, 0), 0))
    return pl.pallas_call(
        _a_attn_kernel,
        out_shape=jax.ShapeDtypeStruct((s, D_MODEL), F32),
        grid=(s // ROW_TILE,),
        in_specs=[pl.BlockSpec(memory_space=pltpu.SMEM), row(A_HEADS * A_HEAD_DIM), prev(nk), row(nk), prev(nv),
                  row(nv), row(D_MODEL), _const_spec(wo.shape)],
        out_specs=row(D_MODEL),
        scratch_shapes=[pltpu.VMEM((ROW_TILE, A_HEADS * A_HEAD_DIM), BF16)],
        compiler_params=_params(("parallel",)),
        name="swa_attention",
    )(sinks, q, k, k, v, v, x, wo)


def _mixer_a(x, g, w_qkv, q_norm, k_norm, sinks, w_o, tables):
    cos, sin, sign = tables
    nq = A_HEADS * A_HEAD_DIM
    gain = jnp.concatenate([jnp.tile(q_norm, A_HEADS), jnp.tile(k_norm, A_KV_HEADS)]).reshape(1, -1)
    q, k, v = _a_proj(x, g.reshape(1, -1), w_qkv.astype(BF16), gain, _block_ones(A_HEAD_DIM, A_HEAD_DIM), cos, sin,
                      sign)
    return _a_attn(sinks, q, k, v, x, w_o.astype(BF16))


def _b_kernel(x_ref, g_ref, win_ref, cw_ref, wout_ref, o_ref, carry_ref):
    @pl.when(pl.program_id(0) == 0)
    def _():
        carry_ref[...] = jnp.zeros_like(carry_ref)

    x = x_ref[...]
    h = _rms(x, g_ref[...]).astype(BF16)
    bcu = _dot(h, win_ref[...])
    z = bcu[:, D_MODEL:2 * D_MODEL] * bcu[:, 2 * D_MODEL:]
    r = lax.broadcasted_iota(jnp.int32, (ROW_TILE, 1), 0)
    z1 = jnp.where(r == 0, carry_ref[7:8, :], pltpu.roll(z, 1, 0))
    z2 = jnp.where(r == 0, carry_ref[6:7, :], jnp.where(r == 1, carry_ref[7:8, :], pltpu.roll(z, 2, 0)))
    y = cw_ref[0:1, :] * z2 + cw_ref[1:2, :] * z1 + cw_ref[2:3, :] * z
    carry_ref[...] = z[ROW_TILE - 8:]
    o_ref[...] = x + _dot((bcu[:, :D_MODEL] * y).astype(BF16), wout_ref[...])


def _mixer_b(x, g, w_in, conv_w, w_out):
    s = x.shape[0]
    row = pl.BlockSpec((ROW_TILE, D_MODEL), lambda i: (i, 0))
    return pl.pallas_call(
        _b_kernel,
        out_shape=jax.ShapeDtypeStruct((s, D_MODEL), F32),
        grid=(s // ROW_TILE,),
        in_specs=[row, _const_spec((1, D_MODEL)), _const_spec(w_in.shape), _const_spec(conv_w.shape),
                  _const_spec(w_out.shape)],
        out_specs=row,
        scratch_shapes=[pltpu.VMEM((8, D_MODEL), F32)],
        compiler_params=_params(("arbitrary",)),
        name="short_conv",
    )(x, g.reshape(1, -1), w_in.astype(BF16), conv_w, w_out.astype(BF16))


def _c_proj_kernel(x_ref, g_ref, wd_ref, qan_ref, kvan_ref, wq_ref, wkv_ref, qn_ref, kn_ref, knp_ref, ones_ref,
                   cos_ref, sin_ref, q_ref, k_ref, v_ref):
    h = _rms(x_ref[...], g_ref[...]).astype(BF16)
    d = _dot(h, wd_ref[...])
    cq = _rms(d[:, :C_Q_RANK], qan_ref[...]).astype(BF16)
    lat = C_Q_RANK + C_KV_RANK
    ckv = _rms(d[:, C_Q_RANK:lat], kvan_ref[...]).astype(BF16)
    kr, krp = d[:, lat:lat + LANES], d[:, lat + LANES:]
    cos, sin = cos_ref[...], sin_ref[...]
    k_rot = (kr * kn_ref[...]) * cos + (krp * knp_ref[...]) * sin
    ss_rope = jnp.sum(kr * kr, axis=-1, keepdims=True)
    q_all = _dot(cq, wq_ref[...])
    kv_all = _dot(ckv, wkv_ref[...])
    q_scale = C_DQK ** -0.5 * LOG2E
    qn2 = jnp.concatenate([qn_ref[...]] * 2, axis=1)
    for j in range(C_HEADS // 2):
        cols = slice(2 * j * LANES, (2 * j + 2) * LANES)
        qb, kb = q_all[:, cols], kv_all[:, cols]
        ssq = _dot((qb * qb).astype(BF16), ones_ref[...])
        ssk = _dot((kb * kb).astype(BF16), ones_ref[...]) + ss_rope
        y = qb * lax.rsqrt(ssq * (1.0 / C_DQK) + EPS) * qn2
        rk = lax.rsqrt(ssk * (1.0 / C_DQK) + EPS)
        for c in range(2):
            lanes = slice(c * LANES, (c + 1) * LANES)
            yc = y[:, lanes]
            yc = yc * cos + pltpu.roll(yc, LANES - C_ROPE // 2, 1) * sin
            q_ref[2 * j + c] = (yc * q_scale).astype(BF16)
            k_ref[2 * j + c] = (rk[:, lanes] * (kb[:, lanes] * kn_ref[...] + k_rot)).astype(BF16)
        v_ref[j] = kv_all[:, (C_HEADS + j) * LANES:(C_HEADS + j + 1) * LANES].astype(BF16)


def _c_proj(x, g, wd, qan, kvan, wq, wkv, qn, kn, knp, ones, cos, sin):
    s = x.shape[0]
    row = lambda n: pl.BlockSpec((ROW_TILE, n), lambda i: (i, 0))
    heads = lambda n: pl.BlockSpec((n, ROW_TILE, LANES), lambda i: (0, i, 0))
    return pl.pallas_call(
        _c_proj_kernel,
        out_shape=(jax.ShapeDtypeStruct((C_HEADS, s, LANES), BF16), jax.ShapeDtypeStruct((C_HEADS, s, LANES), BF16),
                   jax.ShapeDtypeStruct((C_HEADS // 2, s, LANES), BF16)),
        grid=(s // ROW_TILE,),
        in_specs=[row(D_MODEL), _const_spec((1, D_MODEL)), _const_spec(wd.shape), _const_spec(qan.shape),
                  _const_spec(kvan.shape), _const_spec(wq.shape), _const_spec(wkv.shape), _const_spec((1, LANES)),
                  _const_spec((1, LANES)), _const_spec((1, LANES)), _const_spec(ones.shape), row(LANES), row(LANES)],
        out_specs=(heads(C_HEADS), heads(C_HEADS), heads(C_HEADS // 2)),
        compiler_params=_params(("parallel",)),
        name="mla_proj",
    )(x, g, wd, qan, kvan, wq, wkv, qn, kn, knp, ones, cos, sin)


def _c_flash_kernel(qi_ref, ki_ref, last_ref, q_ref, k_ref, v_ref, x_ref, wo_ref, o_ref, m_ref, l_ref, acc_ref,
                    s0_ref, s1_ref, p0_ref, p1_ref, a0_ref, a1_ref):
    step = pl.program_id(0)
    qi, ki = qi_ref[step], ki_ref[step]
    lo = _lane_lo()
    s_bufs, p_bufs, a_bufs = (s0_ref, s1_ref), (p0_ref, p1_ref), (a0_ref, a1_ref)

    @pl.when(ki == 0)
    def _():
        m_ref[...] = jnp.full_like(m_ref, -jnp.inf)
        l_ref[...] = jnp.zeros_like(l_ref)
        acc_ref[...] = jnp.zeros_like(acc_ref)

    def sweep(masked):
        if masked:
            qpos = qi * MLA_TQ + lax.broadcasted_iota(jnp.int32, (MLA_TQ, MLA_TK), 0)
            kpos = ki * MLA_TK + lax.broadcasted_iota(jnp.int32, (MLA_TQ, MLA_TK), 1)
            causal = kpos <= qpos

        def scores(h, slot):
            s = _dot_nt(q_ref[h], k_ref[h])
            if masked:
                s = jnp.where(causal, s, -jnp.inf)
            s_bufs[slot][...] = s

        def softmax(h, slot):
            s_ref, p_ref = s_bufs[slot], p_bufs[slot]
            for rb in range(MLA_TQ // SOFTMAX_ROWS):
                rows = slice(rb * SOFTMAX_ROWS, (rb + 1) * SOFTMAX_ROWS)
                chunks = [slice(c * LANES, (c + 1) * LANES) for c in range(MLA_TK // LANES)]
                part = s_ref[rows, chunks[0]]
                for cols in chunks[1:]:
                    part = jnp.maximum(part, s_ref[rows, cols])
                m_prev = m_ref[h, rows, :]
                m_new = jnp.maximum(m_prev, jnp.max(part, axis=-1, keepdims=True))
                alpha = jnp.exp2(m_prev - m_new)
                lsum = alpha * l_ref[h, rows, :]
                for cols in chunks:
                    p = jnp.exp2(s_ref[rows, cols] - m_new)
                    lsum = lsum + p
                    p_ref[rows, cols] = p.astype(BF16)
                l_ref[h, rows, :] = lsum
                m_ref[h, rows, :] = m_new
                a_bufs[slot][rows, :] = alpha

        def pv(j, slot):
            keep = lo if slot == 0 else jnp.logical_not(lo)
            acc = acc_ref[j]
            acc_ref[j] = jnp.where(keep, acc * a_bufs[slot][...] + _dot(p_bufs[slot][...], v_ref[j]), acc)

        for t in range(C_HEADS + 2):
            if t < C_HEADS:
                scores(t, t % 2)
            if 1 <= t <= C_HEADS:
                softmax(t - 1, (t - 1) % 2)
            if t >= 2:
                pv((t - 2) // 2, t % 2)

    needs_mask = (ki + 1) * MLA_TK - 1 > qi * MLA_TQ

    @pl.when(needs_mask)
    def _():
        sweep(True)

    @pl.when(jnp.logical_not(needs_mask))
    def _():
        sweep(False)

    @pl.when(last_ref[step] == 1)
    def _():
        cols = []
        for j in range(C_HEADS // 2):
            l_even = jnp.sum(l_ref[2 * j], axis=-1, keepdims=True)
            l_odd = jnp.sum(l_ref[2 * j + 1], axis=-1, keepdims=True)
            cols.append((acc_ref[j] / jnp.where(lo, l_even, l_odd)).astype(BF16))
        o_ref[...] = x_ref[...] + _dot(jnp.concatenate(cols, axis=1), wo_ref[...])


def _flash_schedule(s):
    qi, ki, last = [], [], []
    for a in range(s // MLA_TQ):
        nk = -(-((a + 1) * MLA_TQ) // MLA_TK)
        for b in range(nk):
            qi.append(a)
            ki.append(b)
            last.append(int(b == nk - 1))
    return tuple(jnp.asarray(v, jnp.int32) for v in (qi, ki, last))


def _c_flash(q, k, v, x, wo):
    s = x.shape[0]
    qi, ki, last = _flash_schedule(s)
    grid_spec = pltpu.PrefetchScalarGridSpec(
        num_scalar_prefetch=3,
        grid=(int(qi.shape[0]),),
        in_specs=[
            pl.BlockSpec((C_HEADS, MLA_TQ, LANES), lambda t, qi, ki, last: (0, qi[t], 0)),
            pl.BlockSpec((C_HEADS, MLA_TK, LANES), lambda t, qi, ki, last: (0, ki[t], 0)),
            pl.BlockSpec((C_HEADS // 2, MLA_TK, LANES), lambda t, qi, ki, last: (0, ki[t], 0)),
            pl.BlockSpec((MLA_TQ, D_MODEL), lambda t, qi, ki, last: (qi[t], 0)),
            pl.BlockSpec(wo.shape, lambda t, qi, ki, last: (0, 0), pipeline_mode=pl.Buffered(1)),
        ],
        out_specs=pl.BlockSpec((MLA_TQ, D_MODEL), lambda t, qi, ki, last: (qi[t], 0)),
        scratch_shapes=[pltpu.VMEM((C_HEADS, MLA_TQ, LANES), F32), pltpu.VMEM((C_HEADS, MLA_TQ, LANES), F32),
                        pltpu.VMEM((C_HEADS // 2, MLA_TQ, LANES), F32),
                        pltpu.VMEM((MLA_TQ, MLA_TK), F32), pltpu.VMEM((MLA_TQ, MLA_TK), F32),
                        pltpu.VMEM((MLA_TQ, MLA_TK), BF16), pltpu.VMEM((MLA_TQ, MLA_TK), BF16),
                        pltpu.VMEM((MLA_TQ, LANES), F32), pltpu.VMEM((MLA_TQ, LANES), F32)],
    )
    return pl.pallas_call(
        _c_flash_kernel,
        out_shape=jax.ShapeDtypeStruct((s, D_MODEL), F32),
        grid_spec=grid_spec,
        compiler_params=_params(("arbitrary",)),
        name="mla_flash",
    )(qi, ki, last, q, k, v, x, wo)


def _pad_lanes(v, n=LANES):
    return jnp.pad(v, (0, n - v.shape[0])).reshape(1, n)


def _mixer_c(x, g, w_down, q_a_norm, kv_a_norm, w_q_up, w_kv_up, q_norm, k_norm, w_o, tables):
    cos, sin, _ = tables
    half = C_ROPE // 2
    lat = C_Q_RANK + C_KV_RANK
    w_rope = w_down[:, lat:]
    w_rope_partner = jnp.concatenate([w_rope[:, half:], w_rope[:, :half]], axis=1)
    at_rope_lanes = lambda w: jnp.pad(w, ((0, 0), (C_NOPE, LANES - C_DQK)))
    wd = jnp.concatenate([w_down[:, :lat], at_rope_lanes(w_rope), at_rope_lanes(w_rope_partner)], axis=1).astype(BF16)
    wq = w_q_up.reshape(C_Q_RANK, C_HEADS, C_DQK)
    wq = jnp.concatenate([wq, wq[:, :, C_NOPE:C_NOPE + half], jnp.zeros((C_Q_RANK, C_HEADS, half), F32)], axis=2)
    wq = wq.reshape(C_Q_RANK, C_HEADS * LANES).astype(BF16)
    wkv = w_kv_up.reshape(C_KV_RANK, C_HEADS, C_NOPE + C_V)
    wk = jnp.pad(wkv[:, :, :C_NOPE], ((0, 0), (0, 0), (0, LANES - C_NOPE))).reshape(C_KV_RANK, C_HEADS * LANES)
    wv = wkv[:, :, C_NOPE:].reshape(C_KV_RANK, C_HEADS * C_V)
    wkv = jnp.concatenate([wk, wv], axis=1).astype(BF16)
    qn = _pad_lanes(jnp.concatenate([q_norm, q_norm[C_NOPE:C_NOPE + half]]))
    kn_rope = k_norm[C_NOPE:]
    knp = _pad_lanes(jnp.concatenate([jnp.zeros((C_NOPE,), F32), kn_rope[half:], kn_rope[:half]]))
    q, k, v = _c_proj(x, g.reshape(1, -1), wd, q_a_norm.reshape(1, -1), kv_a_norm.reshape(1, -1), wq, wkv,
                      qn, _pad_lanes(k_norm), knp, _block_ones(LANES, C_DQK), cos, sin)
    return _c_flash(q, k, v, x, w_o.astype(BF16))


def _ffn_kernel(x_ref, g_ref, wgu_ref, wd_ref, o_ref):
    x = x_ref[...]
    h = _rms(x, g_ref[...]).astype(BF16)
    d_ff = wd_ref.shape[0]
    acc = x
    for c in range(d_ff // FFN_CHUNK):
        c0 = c * FFN_CHUNK
        gate = _dot(h, wgu_ref[:, c0:c0 + FFN_CHUNK])
        up = _dot(h, wgu_ref[:, d_ff + c0:d_ff + c0 + FFN_CHUNK])
        act = (gate * jax.nn.sigmoid(gate) * up).astype(BF16)
        acc = acc + _dot(act, wd_ref[c0:c0 + FFN_CHUNK, :])
    o_ref[...] = acc


def _ffn(x, g, w_gate_up, w_down):
    s = x.shape[0]
    row = pl.BlockSpec((ROW_TILE, D_MODEL), lambda i: (i, 0))
    return pl.pallas_call(
        _ffn_kernel,
        out_shape=jax.ShapeDtypeStruct((s, D_MODEL), F32),
        grid=(s // ROW_TILE,),
        in_specs=[row, _const_spec((1, D_MODEL)), _const_spec(w_gate_up.shape), _const_spec(w_down.shape)],
        out_specs=row,
        compiler_params=_params(("parallel",)),
        name="swiglu_ffn",
    )(x, g.reshape(1, -1), w_gate_up.astype(BF16), w_down.astype(BF16))


def kernel(x, positions, mix_norm, ffn_norm, a_w_qkv, a_q_norm, a_k_norm, a_sinks, a_w_o, b_w_in, b_conv_w, b_w_out,
           c_w_down, c_q_a_norm, c_kv_a_norm, c_w_q_up, c_w_kv_up, c_q_norm, c_k_norm, c_w_o, f_w_gate_up, f_w_down):
    batch, seq, d = x.shape
    assert batch == 1 and d == D_MODEL and seq % max(ROW_TILE, ROPE_TILE, MLA_TQ, MLA_TK) == 0
    depth = mix_norm.shape[0]
    pos_col = positions.reshape(seq, 1)
    pat_a = _rope_pattern(A_ROT_DIM, 0, A_HEAD_DIM)
    pat_c = _rope_pattern(C_ROPE, C_NOPE, LANES)
    tab_a = _rope_tables(pos_col, *pat_a) + (pat_a[1],)
    tab_c = _rope_tables(pos_col, *pat_c) + (pat_c[1],)
    xs = x.reshape(seq, d)
    for i in range(depth):
        kind, j = i % 3, i // 3
        if kind == 0:
            xs = _mixer_a(xs, mix_norm[i], a_w_qkv[j], a_q_norm[j], a_k_norm[j], a_sinks[j], a_w_o[j], tab_a)
        elif kind == 1:
            xs = _mixer_b(xs, mix_norm[i], b_w_in[j], b_conv_w[j], b_w_out[j])
        else:
            xs = _mixer_c(xs, mix_norm[i], c_w_down[j], c_q_a_norm[j], c_kv_a_norm[j], c_w_q_up[j], c_w_kv_up[j],
                          c_q_norm[j], c_k_norm[j], c_w_o[j], tab_c)
        xs = _ffn(xs, ffn_norm[i], f_w_gate_up[i], f_w_down[i])
    return xs.reshape(batch, seq, d)
```

```python
import math

import jax
import jax.numpy as jnp
import numpy as np
from jax import lax
from jax.experimental import pallas as pl
from jax.experimental.pallas import tpu as pltpu

F32 = jnp.float32
BF16 = jnp.bfloat16

D_MODEL = 1024
ROPE_THETA = 500000.0
EPS = 1e-6
A_HEADS, A_KV_HEADS, A_HEAD_DIM, A_ROT_DIM, A_WINDOW = 16, 4, 64, 16, 128
C_HEADS, C_NOPE, C_ROPE, C_V, C_Q_RANK, C_KV_RANK = 16, 64, 32, 64, 384, 256
C_DQK = C_NOPE + C_ROPE
LOG2E = math.log2(math.e)

LANES = 128
VMEM_LIMIT_BYTES = 56 * 2**20

ROW_TILE = 512
ROPE_TILE = 2048
FFN_CHUNK = 256
MLA_TQ = 512
MLA_TK = 512
SOFTMAX_ROWS = 64


def _params(sem, flags=None):
    return pltpu.CompilerParams(dimension_semantics=sem, vmem_limit_bytes=VMEM_LIMIT_BYTES, flags=flags)


def _const_spec(shape):
    return pl.BlockSpec(shape, lambda *_: (0,) * len(shape), pipeline_mode=pl.Buffered(1))


def _rms(x, g):
    ms = jnp.mean(x * x, axis=-1, keepdims=True)
    return x * lax.rsqrt(ms + EPS) * g


def _dot(a, b):
    return jnp.dot(a, b, preferred_element_type=F32)


def _dot_nt(a, b):
    return lax.dot_general(a, b, (((1,), (1,)), ((), ())), preferred_element_type=F32)


def _lane_lo():
    return lax.broadcasted_iota(jnp.int32, (1, LANES), 1) < (LANES // 2)


def _rope_apply(x, cos, sin, first_half, half):
    fwd = pltpu.roll(x, LANES - half, 1)
    bwd = pltpu.roll(x, half, 1)
    return x * cos + jnp.where(first_half, fwd, bwd) * sin


def _rope_kernel(pos_ref, invf_ref, sign_ref, cos_ref, sin_ref):
    ang = pos_ref[...].astype(F32) * invf_ref[...]
    cos_ref[...] = jnp.cos(ang)
    sin_ref[...] = jnp.sin(ang) * sign_ref[...]


def _rope_tables(pos_col, invf, sign):
    s = pos_col.shape[0]
    row = pl.BlockSpec((1, LANES), lambda i: (0, 0))
    out = pl.BlockSpec((ROPE_TILE, LANES), lambda i: (i, 0))
    return pl.pallas_call(
        _rope_kernel,
        out_shape=(jax.ShapeDtypeStruct((s, LANES), F32),) * 2,
        grid=(s // ROPE_TILE,),
        in_specs=[pl.BlockSpec((ROPE_TILE, 1), lambda i: (i, 0)), row, row],
        out_specs=(out, out),
        compiler_params=_params(("parallel",)),
        name="rope_tables",
    )(pos_col, invf, sign)


def _rope_pattern(rot_dim, offset, period):
    half = rot_dim // 2
    inv_freq = ROPE_THETA ** (-jnp.arange(0, rot_dim, 2, dtype=F32) / rot_dim)
    d = np.arange(LANES) % period - offset
    rotary = (d >= 0) & (d < rot_dim)
    idx = np.where(rotary, d % half, 0)
    invf = jnp.where(jnp.asarray(rotary), inv_freq[idx], 0.0)
    sign = np.where(rotary, np.where(d < half, -1.0, 1.0), 0.0)
    return invf.reshape(1, LANES).astype(F32), jnp.asarray(sign, F32).reshape(1, LANES)


def _ffn_tile(x, g_ref, wgu_ref, wd_ref):
    h = _rms(x, g_ref[...]).astype(BF16)
    d_ff = wd_ref.shape[0]
    acc = x
    for c in range(d_ff // FFN_CHUNK):
        c0 = c * FFN_CHUNK
        gate = _dot(h, wgu_ref[:, c0:c0 + FFN_CHUNK])
        up = _dot(h, wgu_ref[:, d_ff + c0:d_ff + c0 + FFN_CHUNK])
        act = (gate * jax.nn.sigmoid(gate) * up).astype(BF16)
        acc = acc + _dot(act, wd_ref[c0:c0 + FFN_CHUNK, :])
    return acc


def _ffn_specs(ffn):
    g, wgu, wd = ffn
    return [_const_spec(g.shape), _const_spec(wgu.shape), _const_spec(wd.shape)]


def _block_ones(block, valid):
    r = np.arange(2 * LANES)
    same = (r[:, None] // block) == (r[None, :] // block)
    return jnp.asarray(same & ((r % block) < valid)[:, None], BF16)


def _a_proj_kernel(x_ref, g_ref, w_ref, gain_ref, ones_ref, cos_ref, sin_ref, sign_ref, q_ref, k_ref, v_ref):
    lo = _lane_lo()
    nq = A_HEADS * A_HEAD_DIM
    nkv = A_KV_HEADS * A_HEAD_DIM
    cur = slice(0, ROW_TILE)
    h = _rms(x_ref[...], g_ref[...]).astype(BF16)
    qkv = _dot(h, w_ref[...])
    first = sign_ref[...] < 0.0
    cos, sin = cos_ref[...], sin_ref[...]
    q_scale = A_HEAD_DIM ** -0.5 * LOG2E
    half_lanes = LANES // 2

    def split(col):
        a0 = jnp.where(lo, col, 0.0)
        b1 = jnp.where(lo, 0.0, col)
        return a0, pltpu.roll(a0, half_lanes, 1), pltpu.roll(b1, half_lanes, 1), b1

    for pr in range((nq + nkv) // (2 * LANES)):
        blk4 = qkv[:, pr * 2 * LANES:(pr + 1) * 2 * LANES]
        ss = _dot((blk4 * blk4).astype(BF16), ones_ref[...])
        y = blk4 * lax.rsqrt(ss * (1.0 / A_HEAD_DIM) + EPS) * gain_ref[:, pr * 2 * LANES:(pr + 1) * 2 * LANES]
        for c in range(2):
            col = _rope_apply(y[:, c * LANES:(c + 1) * LANES], cos, sin, first, A_ROT_DIM // 2)
            cidx = 2 * pr + c
            if cidx < nq // LANES:
                q_ref[:, cidx * LANES:(cidx + 1) * LANES] = (col * q_scale).astype(BF16)
            else:
                kc = cidx - nq // LANES
                for t, part in enumerate(split(col)):
                    k_ref[cur, (4 * kc + t) * LANES:(4 * kc + t + 1) * LANES] = part.astype(BF16)
    for c in range(nkv // LANES):
        a0, a1, b0, b1 = split(qkv[:, nq + nkv + c * LANES:nq + nkv + (c + 1) * LANES])
        v_ref[cur, (2 * c) * LANES:(2 * c + 1) * LANES] = (a0 + a1).astype(BF16)
        v_ref[cur, (2 * c + 1) * LANES:(2 * c + 2) * LANES] = (b0 + b1).astype(BF16)


def _a_attn_kernel(sink_ref, q_ref, kp_ref, kc_ref, vp_ref, vc_ref, x_ref, wo_ref, gf_ref, wgu_ref, wd_ref, o_ref,
                   oacc_ref):
    i = pl.program_id(0)
    blk = A_WINDOW
    lo = _lane_lo()
    kcat = jnp.concatenate([kp_ref[...], kc_ref[...]], axis=0)
    vcat = jnp.concatenate([vp_ref[...], vc_ref[...]], axis=0)
    qi = lax.broadcasted_iota(jnp.int32, (2 * blk, 2 * blk), 0) & (blk - 1)
    kj = lax.broadcasted_iota(jnp.int32, (2 * blk, 2 * blk), 1)
    first_key = jnp.where(i > 0, 0, blk)
    ones = jnp.ones((2 * blk, LANES), BF16)
    for b in range(ROW_TILE // blk):
        valid = (kj > qi) & (kj <= qi + blk)
        if b == 0:
            valid = valid & (kj >= first_key)
        rows = slice(b * blk, (b + 1) * blk)
        win = slice(b * blk, (b + 2) * blk)
        for g in range(A_KV_HEADS):
            qs = jnp.concatenate([q_ref[rows, (2 * g) * LANES:(2 * g + 1) * LANES],
                                  q_ref[rows, (2 * g + 1) * LANES:(2 * g + 2) * LANES]], axis=0)
            vd = jnp.concatenate([vcat[win, g * LANES:(g + 1) * LANES], ones], axis=1)
            outs = []
            for half in range(2):
                kd = kcat[win, (2 * g + half) * LANES:(2 * g + half + 1) * LANES]
                sink = jnp.concatenate([jnp.full((blk, LANES), sink_ref[4 * g + half] * LOG2E, F32),
                                        jnp.full((blk, LANES), sink_ref[4 * g + 2 + half] * LOG2E, F32)], axis=0)
                s = jnp.where(valid, _dot_nt(qs, kd), -jnp.inf)
                s0, s1 = s[:, :LANES], s[:, LANES:]
                m = jnp.maximum(jnp.max(jnp.maximum(s0, s1), axis=-1, keepdims=True), sink)
                p = jnp.concatenate([jnp.exp2(s0 - m), jnp.exp2(s1 - m)], axis=1).astype(BF16)
                ol = _dot(p, vd)
                outs.append(ol[:, :LANES] / (ol[:, LANES:] + jnp.exp2(sink - m)))
            pair = jnp.where(lo, outs[0], outs[1])
            oacc_ref[rows, (2 * g) * LANES:(2 * g + 1) * LANES] = pair[:blk].astype(BF16)
            oacc_ref[rows, (2 * g + 1) * LANES:(2 * g + 2) * LANES] = pair[blk:].astype(BF16)

    o_ref[...] = _ffn_tile(x_ref[...] + _dot(oacc_ref[...], wo_ref[...]), gf_ref, wgu_ref, wd_ref)


def _mixer_a(x, g, w_qkv, q_norm, k_norm, sinks, w_o, tables, ffn):
    cos, sin, sign = tables
    s = x.shape[0]
    nq, nk, nv = A_HEADS * A_HEAD_DIM, 2 * A_KV_HEADS * LANES, A_KV_HEADS * LANES
    gain = jnp.concatenate([jnp.tile(q_norm, A_HEADS), jnp.tile(k_norm, A_KV_HEADS)]).reshape(1, -1)
    ones = _block_ones(A_HEAD_DIM, A_HEAD_DIM)
    w, wo = w_qkv.astype(BF16), w_o.astype(BF16)
    row = lambda n: pl.BlockSpec((ROW_TILE, n), lambda i: (i, 0))
    q, k, v = pl.pallas_call(
        _a_proj_kernel,
        out_shape=(jax.ShapeDtypeStruct((s, nq), BF16), jax.ShapeDtypeStruct((s, nk), BF16),
                   jax.ShapeDtypeStruct((s, nv), BF16)),
        grid=(s // ROW_TILE,),
        in_specs=[row(D_MODEL), _const_spec((1, D_MODEL)), _const_spec(w.shape), _const_spec(gain.shape),
                  _const_spec(ones.shape), row(LANES), row(LANES), _const_spec((1, LANES))],
        out_specs=(row(nq), row(nk), row(nv)),
        compiler_params=_params(("parallel",)),
        name="swa_qkv_proj",
    )(x, g.reshape(1, -1), w, gain, ones, cos, sin, sign)
    per = ROW_TILE // A_WINDOW
    prev = lambda n: pl.BlockSpec((A_WINDOW, n), lambda i: (jnp.maximum(i * per - 1, 0), 0))
    return pl.pallas_call(
        _a_attn_kernel,
        out_shape=jax.ShapeDtypeStruct((s, D_MODEL), F32),
        grid=(s // ROW_TILE,),
        in_specs=[pl.BlockSpec(memory_space=pltpu.SMEM), row(nq), prev(nk), row(nk), prev(nv), row(nv), row(D_MODEL),
                  _const_spec(wo.shape)] + _ffn_specs(ffn),
        out_specs=row(D_MODEL),
        scratch_shapes=[pltpu.VMEM((ROW_TILE, nq), BF16)],
        compiler_params=_params(("parallel",)),
        name="swa_attention_layer",
    )(sinks, q, k, k, v, v, x, wo, *ffn)


def _b_kernel(x_ref, g_ref, win_ref, cw_ref, wout_ref, gf_ref, wgu_ref, wd_ref, o_ref, carry_ref):
    @pl.when(pl.program_id(0) == 0)
    def _():
        carry_ref[...] = jnp.zeros_like(carry_ref)

    x = x_ref[...]
    h = _rms(x, g_ref[...]).astype(BF16)
    bcu = _dot(h, win_ref[...])
    z = bcu[:, D_MODEL:2 * D_MODEL] * bcu[:, 2 * D_MODEL:]
    r = lax.broadcasted_iota(jnp.int32, (ROW_TILE, 1), 0)
    z1 = jnp.where(r == 0, carry_ref[7:8, :], pltpu.roll(z, 1, 0))
    z2 = jnp.where(r == 0, carry_ref[6:7, :], jnp.where(r == 1, carry_ref[7:8, :], pltpu.roll(z, 2, 0)))
    y = cw_ref[0:1, :] * z2 + cw_ref[1:2, :] * z1 + cw_ref[2:3, :] * z
    carry_ref[...] = z[ROW_TILE - 8:]
    o_ref[...] = _ffn_tile(x + _dot((bcu[:, :D_MODEL] * y).astype(BF16), wout_ref[...]), gf_ref, wgu_ref, wd_ref)


def _mixer_b(x, g, w_in, conv_w, w_out, ffn):
    s = x.shape[0]
    row = pl.BlockSpec((ROW_TILE, D_MODEL), lambda i: (i, 0))
    return pl.pallas_call(
        _b_kernel,
        out_shape=jax.ShapeDtypeStruct((s, D_MODEL), F32),
        grid=(s // ROW_TILE,),
        in_specs=[row, _const_spec((1, D_MODEL)), _const_spec(w_in.shape), _const_spec(conv_w.shape),
                  _const_spec(w_out.shape)] + _ffn_specs(ffn),
        out_specs=row,
        scratch_shapes=[pltpu.VMEM((8, D_MODEL), F32)],
        compiler_params=_params(("arbitrary",)),
        name="short_conv_layer",
    )(x, g.reshape(1, -1), w_in.astype(BF16), conv_w, w_out.astype(BF16), *ffn)


def _c_proj_kernel(x_ref, g_ref, wd_ref, qan_ref, kvan_ref, wq_ref, wkv_ref, qn_ref, kn_ref, knp_ref, ones_ref,
                   cos_ref, sin_ref, q_ref, k_ref, v_ref):
    h = _rms(x_ref[...], g_ref[...]).astype(BF16)
    d = _dot(h, wd_ref[...])
    cq = _rms(d[:, :C_Q_RANK], qan_ref[...]).astype(BF16)
    lat = C_Q_RANK + C_KV_RANK
    ckv = _rms(d[:, C_Q_RANK:lat], kvan_ref[...]).astype(BF16)
    kr, krp = d[:, lat:lat + LANES], d[:, lat + LANES:]
    cos, sin = cos_ref[...], sin_ref[...]
    k_rot = (kr * kn_ref[...]) * cos + (krp * knp_ref[...]) * sin
    ss_rope = jnp.sum(kr * kr, axis=-1, keepdims=True)
    q_all = _dot(cq, wq_ref[...])
    kv_all = _dot(ckv, wkv_ref[...])
    q_scale = C_DQK ** -0.5 * LOG2E
    qn2 = jnp.concatenate([qn_ref[...]] * 2, axis=1)
    for j in range(C_HEADS // 2):
        cols = slice(2 * j * LANES, (2 * j + 2) * LANES)
        qb, kb = q_all[:, cols], kv_all[:, cols]
        ssq = _dot((qb * qb).astype(BF16), ones_ref[...])
        ssk = _dot((kb * kb).astype(BF16), ones_ref[...]) + ss_rope
        y = qb * lax.rsqrt(ssq * (1.0 / C_DQK) + EPS) * qn2
        rk = lax.rsqrt(ssk * (1.0 / C_DQK) + EPS)
        for c in range(2):
            lanes = slice(c * LANES, (c + 1) * LANES)
            yc = y[:, lanes]
            yc = yc * cos + pltpu.roll(yc, LANES - C_ROPE // 2, 1) * sin
            q_ref[2 * j + c] = (yc * q_scale).astype(BF16)
            k_ref[2 * j + c] = (rk[:, lanes] * (kb[:, lanes] * kn_ref[...] + k_rot)).astype(BF16)
        v_ref[j] = kv_all[:, (C_HEADS + j) * LANES:(C_HEADS + j + 1) * LANES].astype(BF16)


def _c_proj(x, g, wd, qan, kvan, wq, wkv, qn, kn, knp, ones, cos, sin):
    s = x.shape[0]
    row = lambda n: pl.BlockSpec((ROW_TILE, n), lambda i: (i, 0))
    heads = lambda n: pl.BlockSpec((n, ROW_TILE, LANES), lambda i: (0, i, 0))
    return pl.pallas_call(
        _c_proj_kernel,
        out_shape=(jax.ShapeDtypeStruct((C_HEADS, s, LANES), BF16), jax.ShapeDtypeStruct((C_HEADS, s, LANES), BF16),
                   jax.ShapeDtypeStruct((C_HEADS // 2, s, LANES), BF16)),
        grid=(s // ROW_TILE,),
        in_specs=[row(D_MODEL), _const_spec((1, D_MODEL)), _const_spec(wd.shape), _const_spec(qan.shape),
                  _const_spec(kvan.shape), _const_spec(wq.shape), _const_spec(wkv.shape), _const_spec((1, LANES)),
                  _const_spec((1, LANES)), _const_spec((1, LANES)), _const_spec(ones.shape), row(LANES), row(LANES)],
        out_specs=(heads(C_HEADS), heads(C_HEADS), heads(C_HEADS // 2)),
        compiler_params=_params(("parallel",)),
        name="mla_proj",
    )(x, g, wd, qan, kvan, wq, wkv, qn, kn, knp, ones, cos, sin)


def _c_flash_kernel(qi_ref, ki_ref, last_ref, q_ref, k_ref, v_ref, x_ref, wo_ref, gf_ref, wgu_ref, wd_ref, o_ref,
                    m_ref, l_ref, acc_ref, s0_ref, s1_ref, p0_ref, p1_ref, a0_ref, a1_ref):
    step = pl.program_id(0)
    qi, ki = qi_ref[step], ki_ref[step]
    lo = _lane_lo()
    s_bufs, p_bufs, a_bufs = (s0_ref, s1_ref), (p0_ref, p1_ref), (a0_ref, a1_ref)

    @pl.when(ki == 0)
    def _():
        m_ref[...] = jnp.full_like(m_ref, -jnp.inf)
        l_ref[...] = jnp.zeros_like(l_ref)
        acc_ref[...] = jnp.zeros_like(acc_ref)

    def sweep(masked):
        if masked:
            qpos = qi * MLA_TQ + lax.broadcasted_iota(jnp.int32, (MLA_TQ, MLA_TK), 0)
            kpos = ki * MLA_TK + lax.broadcasted_iota(jnp.int32, (MLA_TQ, MLA_TK), 1)
            causal = kpos <= qpos

        def scores(h, slot):
            s = _dot_nt(q_ref[h], k_ref[h])
            if masked:
                s = jnp.where(causal, s, -jnp.inf)
            s_bufs[slot][...] = s

        def softmax(h, slot):
            s_ref, p_ref = s_bufs[slot], p_bufs[slot]
            for rb in range(MLA_TQ // SOFTMAX_ROWS):
                rows = slice(rb * SOFTMAX_ROWS, (rb + 1) * SOFTMAX_ROWS)
                chunks = [slice(c * LANES, (c + 1) * LANES) for c in range(MLA_TK // LANES)]
                part = s_ref[rows, chunks[0]]
                for cols in chunks[1:]:
                    part = jnp.maximum(part, s_ref[rows, cols])
                m_prev = m_ref[h, rows, :]
                m_new = jnp.maximum(m_prev, jnp.max(part, axis=-1, keepdims=True))
                alpha = jnp.exp2(m_prev - m_new)
                lsum = alpha * l_ref[h, rows, :]
                for cols in chunks:
                    p = jnp.exp2(s_ref[rows, cols] - m_new)
                    lsum = lsum + p
                    p_ref[rows, cols] = p.astype(BF16)
                l_ref[h, rows, :] = lsum
                m_ref[h, rows, :] = m_new
                a_bufs[slot][rows, :] = alpha

        def pv(j, slot):
            keep = lo if slot == 0 else jnp.logical_not(lo)
            acc = acc_ref[j]
            acc_ref[j] = jnp.where(keep, acc * a_bufs[slot][...] + _dot(p_bufs[slot][...], v_ref[j]), acc)

        for t in range(C_HEADS + 2):
            if t < C_HEADS:
                scores(t, t % 2)
            if 1 <= t <= C_HEADS:
                softmax(t - 1, (t - 1) % 2)
            if t >= 2:
                pv((t - 2) // 2, t % 2)

    needs_mask = (ki + 1) * MLA_TK - 1 > qi * MLA_TQ

    @pl.when(needs_mask)
    def _():
        sweep(True)

    @pl.when(jnp.logical_not(needs_mask))
    def _():
        sweep(False)

    @pl.when(last_ref[step] == 1)
    def _():
        cols = []
        for j in range(C_HEADS // 2):
            l_even = jnp.sum(l_ref[2 * j], axis=-1, keepdims=True)
            l_odd = jnp.sum(l_ref[2 * j + 1], axis=-1, keepdims=True)
            cols.append((acc_ref[j] / jnp.where(lo, l_even, l_odd)).astype(BF16))
        y = x_ref[...] + _dot(jnp.concatenate(cols, axis=1), wo_ref[...])
        o_ref[...] = _ffn_tile(y, gf_ref, wgu_ref, wd_ref)


def _flash_schedule(s):
    qi, ki, last = [], [], []
    for a in range(s // MLA_TQ):
        nk = -(-((a + 1) * MLA_TQ) // MLA_TK)
        for b in range(nk):
            qi.append(a)
            ki.append(b)
            last.append(int(b == nk - 1))
    return tuple(jnp.asarray(v, jnp.int32) for v in (qi, ki, last))


def _c_flash(q, k, v, x, wo, ffn):
    s = x.shape[0]
    qi, ki, last = _flash_schedule(s)
    const = lambda a: pl.BlockSpec(a.shape, lambda t, qi, ki, last: (0,) * a.ndim, pipeline_mode=pl.Buffered(1))
    grid_spec = pltpu.PrefetchScalarGridSpec(
        num_scalar_prefetch=3,
        grid=(int(qi.shape[0]),),
        in_specs=[
            pl.BlockSpec((C_HEADS, MLA_TQ, LANES), lambda t, qi, ki, last: (0, qi[t], 0), pipeline_mode=pl.Buffered(1)),
            pl.BlockSpec((C_HEADS, MLA_TK, LANES), lambda t, qi, ki, last: (0, ki[t], 0)),
            pl.BlockSpec((C_HEADS // 2, MLA_TK, LANES), lambda t, qi, ki, last: (0, ki[t], 0)),
            pl.BlockSpec((MLA_TQ, D_MODEL), lambda t, qi, ki, last: (qi[t], 0), pipeline_mode=pl.Buffered(1)),
            const(wo), const(ffn[0]), const(ffn[1]), const(ffn[2]),
        ],
        out_specs=pl.BlockSpec((MLA_TQ, D_MODEL), lambda t, qi, ki, last: (qi[t], 0)),
        scratch_shapes=[pltpu.VMEM((C_HEADS, MLA_TQ, LANES), F32), pltpu.VMEM((C_HEADS, MLA_TQ, LANES), F32),
                        pltpu.VMEM((C_HEADS // 2, MLA_TQ, LANES), F32),
                        pltpu.VMEM((MLA_TQ, MLA_TK), F32), pltpu.VMEM((MLA_TQ, MLA_TK), F32),
                        pltpu.VMEM((MLA_TQ, MLA_TK), BF16), pltpu.VMEM((MLA_TQ, MLA_TK), BF16),
                        pltpu.VMEM((MLA_TQ, LANES), F32), pltpu.VMEM((MLA_TQ, LANES), F32)],
    )
    return pl.pallas_call(
        _c_flash_kernel,
        out_shape=jax.ShapeDtypeStruct((s, D_MODEL), F32),
        grid_spec=grid_spec,
        compiler_params=_params(("arbitrary",)),
        name="mla_flash",
    )(qi, ki, last, q, k, v, x, wo, *ffn)


def _pad_lanes(v, n=LANES):
    return jnp.pad(v, (0, n - v.shape[0])).reshape(1, n)


def _mixer_c(x, g, w_down, q_a_norm, kv_a_norm, w_q_up, w_kv_up, q_norm, k_norm, w_o, tables, ffn):
    cos, sin, _ = tables
    half = C_ROPE // 2
    lat = C_Q_RANK + C_KV_RANK
    w_rope = w_down[:, lat:]
    w_rope_partner = jnp.concatenate([w_rope[:, half:], w_rope[:, :half]], axis=1)
    at_rope_lanes = lambda w: jnp.pad(w, ((0, 0), (C_NOPE, LANES - C_DQK)))
    wd = jnp.concatenate([w_down[:, :lat], at_rope_lanes(w_rope), at_rope_lanes(w_rope_partner)], axis=1).astype(BF16)
    wq = w_q_up.reshape(C_Q_RANK, C_HEADS, C_DQK)
    wq = jnp.concatenate([wq, wq[:, :, C_NOPE:C_NOPE + half], jnp.zeros((C_Q_RANK, C_HEADS, half), F32)], axis=2)
    wq = wq.reshape(C_Q_RANK, C_HEADS * LANES).astype(BF16)
    wkv = w_kv_up.reshape(C_KV_RANK, C_HEADS, C_NOPE + C_V)
    wk = jnp.pad(wkv[:, :, :C_NOPE], ((0, 0), (0, 0), (0, LANES - C_NOPE))).reshape(C_KV_RANK, C_HEADS * LANES)
    wv = wkv[:, :, C_NOPE:].reshape(C_KV_RANK, C_HEADS * C_V)
    wkv = jnp.concatenate([wk, wv], axis=1).astype(BF16)
    qn = _pad_lanes(jnp.concatenate([q_norm, q_norm[C_NOPE:C_NOPE + half]]))
    kn_rope = k_norm[C_NOPE:]
    knp = _pad_lanes(jnp.concatenate([jnp.zeros((C_NOPE,), F32), kn_rope[half:], kn_rope[:half]]))
    q, k, v = _c_proj(x, g.reshape(1, -1), wd, q_a_norm.reshape(1, -1), kv_a_norm.reshape(1, -1), wq, wkv,
                      qn, _pad_lanes(k_norm), knp, _block_ones(LANES, C_DQK), cos, sin)
    return _c_flash(q, k, v, x, w_o.astype(BF16), ffn)


def kernel(x, positions, mix_norm, ffn_norm, a_w_qkv, a_q_norm, a_k_norm, a_sinks, a_w_o, b_w_in, b_conv_w, b_w_out,
           c_w_down, c_q_a_norm, c_kv_a_norm, c_w_q_up, c_w_kv_up, c_q_norm, c_k_norm, c_w_o, f_w_gate_up, f_w_down):
    batch, seq, d = x.shape
    assert batch == 1 and d == D_MODEL and seq % max(ROW_TILE, ROPE_TILE, MLA_TQ, MLA_TK) == 0
    depth = mix_norm.shape[0]
    pos_col = positions.reshape(seq, 1)
    pat_a = _rope_pattern(A_ROT_DIM, 0, A_HEAD_DIM)
    pat_c = _rope_pattern(C_ROPE, C_NOPE, LANES)
    tab_a = _rope_tables(pos_col, *pat_a) + (pat_a[1],)
    tab_c = _rope_tables(pos_col, *pat_c) + (pat_c[1],)
    xs = x.reshape(seq, d)
    for i in range(depth):
        kind, j = i % 3, i // 3
        ffn = (ffn_norm[i].reshape(1, -1), f_w_gate_up[i].astype(BF16), f_w_down[i].astype(BF16))
        if kind == 0:
            xs = _mixer_a(xs, mix_norm[i], a_w_qkv[j], a_q_norm[j], a_k_norm[j], a_sinks[j], a_w_o[j], tab_a, ffn)
        elif kind == 1:
            xs = _mixer_b(xs, mix_norm[i], b_w_in[j], b_conv_w[j], b_w_out[j], ffn)
        else:
            xs = _mixer_c(xs, mix_norm[i], c_w_down[j], c_q_a_norm[j], c_kv_a_norm[j], c_w_q_up[j], c_w_kv_up[j],
                          c_q_norm[j], c_k_norm[j], c_w_o[j], tab_c, ffn)
    return xs.reshape(batch, seq, d)
```

```python
import math

import jax
import jax.numpy as jnp
import numpy as np
from jax import lax
from jax.experimental import pallas as pl
from jax.experimental.pallas import tpu as pltpu

F32 = jnp.float32
BF16 = jnp.bfloat16

D_MODEL = 1024
ROPE_THETA = 500000.0
EPS = 1e-6
A_HEADS, A_KV_HEADS, A_HEAD_DIM, A_ROT_DIM, A_WINDOW = 16, 4, 64, 16, 128
C_HEADS, C_NOPE, C_ROPE, C_V, C_Q_RANK, C_KV_RANK = 16, 64, 32, 64, 384, 256
C_DQK = C_NOPE + C_ROPE
LOG2E = math.log2(math.e)

LANES = 128
VMEM_LIMIT_BYTES = 56 * 2**20

ROW_TILE = 512
ROPE_TILE = 2048
FFN_CHUNK = 256
MLA_TQ = 512
MLA_TK = 512
SOFTMAX_ROWS = 64


def _params(sem, flags=None):
    return pltpu.CompilerParams(dimension_semantics=sem, vmem_limit_bytes=VMEM_LIMIT_BYTES, flags=flags)


def _const_spec(shape):
    return pl.BlockSpec(shape, lambda *_: (0,) * len(shape), pipeline_mode=pl.Buffered(1))


def _rms(x, g):
    ms = jnp.mean(x * x, axis=-1, keepdims=True)
    return x * lax.rsqrt(ms + EPS) * g


def _dot(a, b):
    return jnp.dot(a, b, preferred_element_type=F32)


def _dot_nt(a, b):
    return lax.dot_general(a, b, (((1,), (1,)), ((), ())), preferred_element_type=F32)


def _lane_lo():
    return lax.broadcasted_iota(jnp.int32, (1, LANES), 1) < (LANES // 2)


def _rope_apply(x, cos, sin, first_half, half):
    fwd = pltpu.roll(x, LANES - half, 1)
    bwd = pltpu.roll(x, half, 1)
    return x * cos + jnp.where(first_half, fwd, bwd) * sin


def _rope_kernel(pos_ref, invf_ref, sign_ref, cos_ref, sin_ref):
    ang = pos_ref[...].astype(F32) * invf_ref[...]
    cos_ref[...] = jnp.cos(ang)
    sin_ref[...] = jnp.sin(ang) * sign_ref[...]


def _rope_tables(pos_col, invf, sign):
    s = pos_col.shape[0]
    row = pl.BlockSpec((1, LANES), lambda i: (0, 0))
    out = pl.BlockSpec((ROPE_TILE, LANES), lambda i: (i, 0))
    return pl.pallas_call(
        _rope_kernel,
        out_shape=(jax.ShapeDtypeStruct((s, LANES), F32),) * 2,
        grid=(s // ROPE_TILE,),
        in_specs=[pl.BlockSpec((ROPE_TILE, 1), lambda i: (i, 0)), row, row],
        out_specs=(out, out),
        compiler_params=_params(("parallel",)),
        name="rope_tables",
    )(pos_col, invf, sign)


def _rope_pattern(rot_dim, offset, period):
    half = rot_dim // 2
    inv_freq = ROPE_THETA ** (-jnp.arange(0, rot_dim, 2, dtype=F32) / rot_dim)
    d = np.arange(LANES) % period - offset
    rotary = (d >= 0) & (d < rot_dim)
    idx = np.where(rotary, d % half, 0)
    invf = jnp.where(jnp.asarray(rotary), inv_freq[idx], 0.0)
    sign = np.where(rotary, np.where(d < half, -1.0, 1.0), 0.0)
    return invf.reshape(1, LANES).astype(F32), jnp.asarray(sign, F32).reshape(1, LANES)


def _ffn_tile(x, g_ref, wgu_ref, wd_ref):
    h = _rms(x, g_ref[...]).astype(BF16)
    d_ff = wd_ref.shape[0]
    acc = x
    for c in range(d_ff // FFN_CHUNK):
        c0 = c * FFN_CHUNK
        gate = _dot(h, wgu_ref[:, c0:c0 + FFN_CHUNK])
        up = _dot(h, wgu_ref[:, d_ff + c0:d_ff + c0 + FFN_CHUNK])
        act = (gate * jax.nn.sigmoid(gate) * up).astype(BF16)
        acc = acc + _dot(act, wd_ref[c0:c0 + FFN_CHUNK, :])
    return acc


def _ffn_specs(ffn):
    g, wgu, wd = ffn
    return [_const_spec(g.shape), _const_spec(wgu.shape), _const_spec(wd.shape)]


def _ffn_kernel(x_ref, g_ref, wgu_ref, wd_ref, o_ref):
    o_ref[...] = _ffn_tile(x_ref[...], g_ref, wgu_ref, wd_ref)


def _ffn(x, ffn):
    s = x.shape[0]
    row = pl.BlockSpec((ROW_TILE, D_MODEL), lambda i: (i, 0))
    return pl.pallas_call(
        _ffn_kernel,
        out_shape=jax.ShapeDtypeStruct((s, D_MODEL), F32),
        grid=(s // ROW_TILE,),
        in_specs=[row] + _ffn_specs(ffn),
        out_specs=row,
        compiler_params=_params(("parallel",)),
        name="swiglu_ffn",
    )(x, *ffn)


def _block_ones(block, valid):
    r = np.arange(2 * LANES)
    same = (r[:, None] // block) == (r[None, :] // block)
    return jnp.asarray(same & ((r % block) < valid)[:, None], BF16)


def _a_proj_kernel(x_ref, g_ref, w_ref, gain_ref, ones_ref, cos_ref, sin_ref, sign_ref, q_ref, k_ref, v_ref, raw_ref):
    @pl.when(pl.program_id(0) == 0)
    def _():
        raw_ref[...] = jnp.zeros_like(raw_ref)

    lo = _lane_lo()
    nq = A_HEADS * A_HEAD_DIM
    nkv = A_KV_HEADS * A_HEAD_DIM
    cur = slice(0, ROW_TILE)
    qkv_cols = lambda c0, n: raw_ref[:, c0:c0 + n]
    first = sign_ref[...] < 0.0
    cos, sin = cos_ref[...], sin_ref[...]
    q_scale = A_HEAD_DIM ** -0.5 * LOG2E
    half_lanes = LANES // 2

    def split(col):
        a0 = jnp.where(lo, col, 0.0)
        b1 = jnp.where(lo, 0.0, col)
        return a0, pltpu.roll(a0, half_lanes, 1), pltpu.roll(b1, half_lanes, 1), b1

    for pr in range((nq + nkv) // (2 * LANES)):
        blk4 = qkv_cols(pr * 2 * LANES, 2 * LANES)
        ss = _dot((blk4 * blk4).astype(BF16), ones_ref[...])
        y = blk4 * lax.rsqrt(ss * (1.0 / A_HEAD_DIM) + EPS) * gain_ref[:, pr * 2 * LANES:(pr + 1) * 2 * LANES]
        for c in range(2):
            col = _rope_apply(y[:, c * LANES:(c + 1) * LANES], cos, sin, first, A_ROT_DIM // 2)
            cidx = 2 * pr + c
            if cidx < nq // LANES:
                q_ref[:, cidx * LANES:(cidx + 1) * LANES] = (col * q_scale).astype(BF16)
            else:
                kc = cidx - nq // LANES
                for t, part in enumerate(split(col)):
                    k_ref[cur, (4 * kc + t) * LANES:(4 * kc + t + 1) * LANES] = part.astype(BF16)
    vals = qkv_cols(nq + nkv, nkv)
    for c in range(nkv // LANES):
        a0, a1, b0, b1 = split(vals[:, c * LANES:(c + 1) * LANES])
        v_ref[cur, (2 * c) * LANES:(2 * c + 1) * LANES] = (a0 + a1).astype(BF16)
        v_ref[cur, (2 * c + 1) * LANES:(2 * c + 2) * LANES] = (b0 + b1).astype(BF16)
    raw_ref[...] = _dot(_rms(x_ref[...], g_ref[...]).astype(BF16), w_ref[...])


def _a_attn_kernel(sink_ref, q_ref, kp_ref, kc_ref, vp_ref, vc_ref, x_ref, wo_ref, gf_ref, wgu_ref, wd_ref, o_ref,
                   oacc_ref):
    i = pl.program_id(0)
    blk = A_WINDOW
    lo = _lane_lo()
    kcat = jnp.concatenate([kp_ref[...], kc_ref[...]], axis=0)
    vcat = jnp.concatenate([vp_ref[...], vc_ref[...]], axis=0)
    qi = lax.broadcasted_iota(jnp.int32, (2 * blk, 2 * blk), 0) & (blk - 1)
    kj = lax.broadcasted_iota(jnp.int32, (2 * blk, 2 * blk), 1)
    first_key = jnp.where(i > 0, 0, blk)
    ones = jnp.ones((2 * blk, LANES), BF16)
    for b in range(ROW_TILE // blk):
        valid = (kj > qi) & (kj <= qi + blk)
        if b == 0:
            valid = valid & (kj >= first_key)
        rows = slice(b * blk, (b + 1) * blk)
        win = slice(b * blk, (b + 2) * blk)
        for g in range(A_KV_HEADS):
            qs = jnp.concatenate([q_ref[rows, (2 * g) * LANES:(2 * g + 1) * LANES],
                                  q_ref[rows, (2 * g + 1) * LANES:(2 * g + 2) * LANES]], axis=0)
            vd = jnp.concatenate([vcat[win, g * LANES:(g + 1) * LANES], ones], axis=1)
            outs = []
            for half in range(2):
                kd = kcat[win, (2 * g + half) * LANES:(2 * g + half + 1) * LANES]
                sink = jnp.concatenate([jnp.full((blk, LANES), sink_ref[4 * g + half] * LOG2E, F32),
                                        jnp.full((blk, LANES), sink_ref[4 * g + 2 + half] * LOG2E, F32)], axis=0)
                s = jnp.where(valid, _dot_nt(qs, kd), -jnp.inf)
                s0, s1 = s[:, :LANES], s[:, LANES:]
                m = jnp.maximum(jnp.max(jnp.maximum(s0, s1), axis=-1, keepdims=True), sink)
                p = jnp.concatenate([jnp.exp2(s0 - m), jnp.exp2(s1 - m)], axis=1).astype(BF16)
                ol = _dot(p, vd)
                outs.append(ol[:, :LANES] / (ol[:, LANES:] + jnp.exp2(sink - m)))
            pair = jnp.where(lo, outs[0], outs[1])
            oacc_ref[rows, (2 * g) * LANES:(2 * g + 1) * LANES] = pair[:blk].astype(BF16)
            oacc_ref[rows, (2 * g + 1) * LANES:(2 * g + 2) * LANES] = pair[blk:].astype(BF16)

    o_ref[...] = _ffn_tile(x_ref[...] + _dot(oacc_ref[...], wo_ref[...]), gf_ref, wgu_ref, wd_ref)


def _mixer_a(x, g, w_qkv, q_norm, k_norm, sinks, w_o, tables, ffn):
    cos, sin, sign = tables
    s = x.shape[0]
    nq, nk, nv = A_HEADS * A_HEAD_DIM, 2 * A_KV_HEADS * LANES, A_KV_HEADS * LANES
    gain = jnp.concatenate([jnp.tile(q_norm, A_HEADS), jnp.tile(k_norm, A_KV_HEADS)]).reshape(1, -1)
    ones = _block_ones(A_HEAD_DIM, A_HEAD_DIM)
    w, wo = w_qkv.astype(BF16), w_o.astype(BF16)
    row = lambda n: pl.BlockSpec((ROW_TILE, n), lambda i: (i, 0))
    tiles = s // ROW_TILE
    ahead = lambda n: pl.BlockSpec((ROW_TILE, n), lambda i: (jnp.minimum(i, tiles - 1), 0))
    behind = lambda n: pl.BlockSpec((ROW_TILE, n), lambda i: (jnp.maximum(i - 1, 0), 0))
    q, k, v = pl.pallas_call(
        _a_proj_kernel,
        out_shape=(jax.ShapeDtypeStruct((s, nq), BF16), jax.ShapeDtypeStruct((s, nk), BF16),
                   jax.ShapeDtypeStruct((s, nv), BF16)),
        grid=(tiles + 1,),
        in_specs=[ahead(D_MODEL), _const_spec((1, D_MODEL)), _const_spec(w.shape), _const_spec(gain.shape),
                  _const_spec(ones.shape), behind(LANES), behind(LANES), _const_spec((1, LANES))],
        out_specs=(behind(nq), behind(nk), behind(nv)),
        scratch_shapes=[pltpu.VMEM((ROW_TILE, w.shape[1]), F32)],
        compiler_params=_params(("arbitrary",)),
        name="swa_qkv_proj",
    )(x, g.reshape(1, -1), w, gain, ones, cos, sin, sign)
    per = ROW_TILE // A_WINDOW
    prev = lambda n: pl.BlockSpec((A_WINDOW, n), lambda i: (jnp.maximum(i * per - 1, 0), 0))
    return pl.pallas_call(
        _a_attn_kernel,
        out_shape=jax.ShapeDtypeStruct((s, D_MODEL), F32),
        grid=(s // ROW_TILE,),
        in_specs=[pl.BlockSpec(memory_space=pltpu.SMEM), row(nq), prev(nk), row(nk), prev(nv), row(nv), row(D_MODEL),
                  _const_spec(wo.shape)] + _ffn_specs(ffn),
        out_specs=row(D_MODEL),
        scratch_shapes=[pltpu.VMEM((ROW_TILE, nq), BF16)],
        compiler_params=_params(("parallel",)),
        name="swa_attention_layer",
    )(sinks, q, k, k, v, v, x, wo, *ffn)


def _b_kernel(x_ref, g_ref, win_ref, cw_ref, wout_ref, gf_ref, wgu_ref, wd_ref, o_ref, carry_ref):
    @pl.when(pl.program_id(0) == 0)
    def _():
        carry_ref[...] = jnp.zeros_like(carry_ref)

    x = x_ref[...]
    h = _rms(x, g_ref[...]).astype(BF16)
    bcu = _dot(h, win_ref[...])
    z = bcu[:, D_MODEL:2 * D_MODEL] * bcu[:, 2 * D_MODEL:]
    r = lax.broadcasted_iota(jnp.int32, (ROW_TILE, 1), 0)
    z1 = jnp.where(r == 0, carry_ref[7:8, :], pltpu.roll(z, 1, 0))
    z2 = jnp.where(r == 0, carry_ref[6:7, :], jnp.where(r == 1, carry_ref[7:8, :], pltpu.roll(z, 2, 0)))
    y = cw_ref[0:1, :] * z2 + cw_ref[1:2, :] * z1 + cw_ref[2:3, :] * z
    carry_ref[...] = z[ROW_TILE - 8:]
    o_ref[...] = _ffn_tile(x + _dot((bcu[:, :D_MODEL] * y).astype(BF16), wout_ref[...]), gf_ref, wgu_ref, wd_ref)


def _mixer_b(x, g, w_in, conv_w, w_out, ffn):
    s = x.shape[0]
    row = pl.BlockSpec((ROW_TILE, D_MODEL), lambda i: (i, 0))
    return pl.pallas_call(
        _b_kernel,
        out_shape=jax.ShapeDtypeStruct((s, D_MODEL), F32),
        grid=(s // ROW_TILE,),
        in_specs=[row, _const_spec((1, D_MODEL)), _const_spec(w_in.shape), _const_spec(conv_w.shape),
                  _const_spec(w_out.shape)] + _ffn_specs(ffn),
        out_specs=row,
        scratch_shapes=[pltpu.VMEM((8, D_MODEL), F32)],
        compiler_params=_params(("arbitrary",)),
        name="short_conv_layer",
    )(x, g.reshape(1, -1), w_in.astype(BF16), conv_w, w_out.astype(BF16), *ffn)


def _c_proj_kernel(x_ref, g_ref, wd_ref, qan_ref, kvan_ref, wq_ref, wkv_ref, qn_ref, kn_ref, knp_ref, ones_ref,
                   cos_ref, sin_ref, q_ref, k_ref, v_ref):
    h = _rms(x_ref[...], g_ref[...]).astype(BF16)
    d = _dot(h, wd_ref[...])
    cq = _rms(d[:, :C_Q_RANK], qan_ref[...]).astype(BF16)
    lat = C_Q_RANK + C_KV_RANK
    ckv = _rms(d[:, C_Q_RANK:lat], kvan_ref[...]).astype(BF16)
    kr, krp = d[:, lat:lat + LANES], d[:, lat + LANES:]
    cos, sin = cos_ref[...], sin_ref[...]
    k_rot = (kr * kn_ref[...]) * cos + (krp * knp_ref[...]) * sin
    ss_rope = jnp.sum(kr * kr, axis=-1, keepdims=True)
    q_scale = C_DQK ** -0.5 * LOG2E
    qn2 = jnp.concatenate([qn_ref[...]] * 2, axis=1)
    for j in range(C_HEADS // 2):
        if j % 2 == 0:
            cols4 = slice(2 * j * LANES, (2 * j + 4) * LANES)
            qb4 = _dot(cq, wq_ref[:, cols4])
            kb4 = _dot(ckv, wkv_ref[:, cols4])
        half4 = slice((j % 2) * 2 * LANES, (j % 2 + 1) * 2 * LANES)
        qb, kb = qb4[:, half4], kb4[:, half4]
        ssq = _dot((qb * qb).astype(BF16), ones_ref[...])
        ssk = _dot((kb * kb).astype(BF16), ones_ref[...]) + ss_rope
        y = qb * lax.rsqrt(ssq * (1.0 / C_DQK) + EPS) * qn2
        rk = lax.rsqrt(ssk * (1.0 / C_DQK) + EPS)
        for c in range(2):
            lanes = slice(c * LANES, (c + 1) * LANES)
            yc = y[:, lanes]
            yc = yc * cos + pltpu.roll(yc, LANES - C_ROPE // 2, 1) * sin
            q_ref[2 * j + c] = (yc * q_scale).astype(BF16)
            k_ref[2 * j + c] = (rk[:, lanes] * (kb[:, lanes] * kn_ref[...] + k_rot)).astype(BF16)
    v_all = _dot(ckv, wkv_ref[:, C_HEADS * LANES:])
    for j in range(C_HEADS // 2):
        v_ref[j] = v_all[:, j * LANES:(j + 1) * LANES].astype(BF16)


def _c_proj(x, g, wd, qan, kvan, wq, wkv, qn, kn, knp, ones, cos, sin):
    s = x.shape[0]
    row = lambda n: pl.BlockSpec((ROW_TILE, n), lambda i: (i, 0))
    heads = lambda n: pl.BlockSpec((n, ROW_TILE, LANES), lambda i: (0, i, 0))
    return pl.pallas_call(
        _c_proj_kernel,
        out_shape=(jax.ShapeDtypeStruct((C_HEADS, s, LANES), BF16), jax.ShapeDtypeStruct((C_HEADS, s, LANES), BF16),
                   jax.ShapeDtypeStruct((C_HEADS // 2, s, LANES), BF16)),
        grid=(s // ROW_TILE,),
        in_specs=[row(D_MODEL), _const_spec((1, D_MODEL)), _const_spec(wd.shape), _const_spec(qan.shape),
                  _const_spec(kvan.shape), _const_spec(wq.shape), _const_spec(wkv.shape), _const_spec((1, LANES)),
                  _const_spec((1, LANES)), _const_spec((1, LANES)), _const_spec(ones.shape), row(LANES), row(LANES)],
        out_specs=(heads(C_HEADS), heads(C_HEADS), heads(C_HEADS // 2)),
        compiler_params=_params(("parallel",)),
        name="mla_proj",
    )(x, g, wd, qan, kvan, wq, wkv, qn, kn, knp, ones, cos, sin)


def _c_flash_kernel(qi_ref, ki_ref, last_ref, q_ref, k_ref, v_ref, x_ref, wo_ref, o_ref,
                    m_ref, l_ref, acc_ref, s0_ref, s1_ref, p0_ref, p1_ref, a0_ref, a1_ref):
    step = pl.program_id(0)
    qi, ki = qi_ref[step], ki_ref[step]
    lo = _lane_lo()
    s_bufs, p_bufs, a_bufs = (s0_ref, s1_ref), (p0_ref, p1_ref), (a0_ref, a1_ref)

    @pl.when(ki == 0)
    def _():
        m_ref[...] = jnp.full_like(m_ref, -jnp.inf)
        l_ref[...] = jnp.zeros_like(l_ref)
        acc_ref[...] = jnp.zeros_like(acc_ref)

    def sweep(masked):
        if masked:
            qpos = qi * MLA_TQ + lax.broadcasted_iota(jnp.int32, (MLA_TQ, MLA_TK), 0)
            kpos = ki * MLA_TK + lax.broadcasted_iota(jnp.int32, (MLA_TQ, MLA_TK), 1)
            causal = kpos <= qpos

        def scores(h, slot):
            s = _dot_nt(q_ref[h], k_ref[h])
            if masked:
                s = jnp.where(causal, s, -jnp.inf)
            s_bufs[slot][...] = s

        def softmax(h, slot):
            s_ref, p_ref = s_bufs[slot], p_bufs[slot]
            for rb in range(MLA_TQ // SOFTMAX_ROWS):
                rows = slice(rb * SOFTMAX_ROWS, (rb + 1) * SOFTMAX_ROWS)
                chunks = [slice(c * LANES, (c + 1) * LANES) for c in range(MLA_TK // LANES)]
                part = s_ref[rows, chunks[0]]
                for cols in chunks[1:]:
                    part = jnp.maximum(part, s_ref[rows, cols])
                m_prev = m_ref[h, rows, :]
                m_new = jnp.maximum(m_prev, jnp.max(part, axis=-1, keepdims=True))
                alpha = jnp.exp2(m_prev - m_new)
                lsum = alpha * l_ref[h, rows, :]
                for cols in chunks:
                    p = jnp.exp2(s_ref[rows, cols] - m_new)
                    lsum = lsum + p
                    p_ref[rows, cols] = p.astype(BF16)
                l_ref[h, rows, :] = lsum
                m_ref[h, rows, :] = m_new
                a_bufs[slot][rows, :] = alpha

        def pv(j, slot):
            keep = lo if slot == 0 else jnp.logical_not(lo)
            acc = acc_ref[j]
            acc_ref[j] = jnp.where(keep, acc * a_bufs[slot][...] + _dot(p_bufs[slot][...], v_ref[j]), acc)

        for t in range(C_HEADS + 2):
            if t < C_HEADS:
                scores(t, t % 2)
            if 1 <= t <= C_HEADS:
                softmax(t - 1, (t - 1) % 2)
            if t >= 2:
                pv((t - 2) // 2, t % 2)

    needs_mask = (ki + 1) * MLA_TK - 1 > qi * MLA_TQ

    @pl.when(needs_mask)
    def _():
        sweep(True)

    @pl.when(jnp.logical_not(needs_mask))
    def _():
        sweep(False)

    @pl.when(last_ref[step] == 1)
    def _():
        cols = []
        for j in range(C_HEADS // 2):
            l_even = jnp.sum(l_ref[2 * j], axis=-1, keepdims=True)
            l_odd = jnp.sum(l_ref[2 * j + 1], axis=-1, keepdims=True)
            cols.append((acc_ref[j] / jnp.where(lo, l_even, l_odd)).astype(BF16))
        o_ref[...] = x_ref[...] + _dot(jnp.concatenate(cols, axis=1), wo_ref[...])


def _flash_schedule(s):
    qi, ki, last = [], [], []
    for a in range(s // MLA_TQ):
        nk = -(-((a + 1) * MLA_TQ) // MLA_TK)
        for b in range(nk):
            qi.append(a)
            ki.append(b)
            last.append(int(b == nk - 1))
    return tuple(jnp.asarray(v, jnp.int32) for v in (qi, ki, last))


def _c_flash(q, k, v, x, wo):
    s = x.shape[0]
    qi, ki, last = _flash_schedule(s)
    grid_spec = pltpu.PrefetchScalarGridSpec(
        num_scalar_prefetch=3,
        grid=(int(qi.shape[0]),),
        in_specs=[
            pl.BlockSpec((C_HEADS, MLA_TQ, LANES), lambda t, qi, ki, last: (0, qi[t], 0)),
            pl.BlockSpec((C_HEADS, MLA_TK, LANES), lambda t, qi, ki, last: (0, ki[t], 0)),
            pl.BlockSpec((C_HEADS // 2, MLA_TK, LANES), lambda t, qi, ki, last: (0, ki[t], 0)),
            pl.BlockSpec((MLA_TQ, D_MODEL), lambda t, qi, ki, last: (qi[t], 0)),
            pl.BlockSpec(wo.shape, lambda t, qi, ki, last: (0, 0), pipeline_mode=pl.Buffered(1)),
        ],
        out_specs=pl.BlockSpec((MLA_TQ, D_MODEL), lambda t, qi, ki, last: (qi[t], 0)),
        scratch_shapes=[pltpu.VMEM((C_HEADS, MLA_TQ, LANES), F32), pltpu.VMEM((C_HEADS, MLA_TQ, LANES), F32),
                        pltpu.VMEM((C_HEADS // 2, MLA_TQ, LANES), F32),
                        pltpu.VMEM((MLA_TQ, MLA_TK), F32), pltpu.VMEM((MLA_TQ, MLA_TK), F32),
                        pltpu.VMEM((MLA_TQ, MLA_TK), BF16), pltpu.VMEM((MLA_TQ, MLA_TK), BF16),
                        pltpu.VMEM((MLA_TQ, LANES), F32), pltpu.VMEM((MLA_TQ, LANES), F32)],
    )
    return pl.pallas_call(
        _c_flash_kernel,
        out_shape=jax.ShapeDtypeStruct((s, D_MODEL), F32),
        grid_spec=grid_spec,
        compiler_params=_params(("arbitrary",)),
        name="mla_flash",
    )(qi, ki, last, q, k, v, x, wo)


def _pad_lanes(v, n=LANES):
    return jnp.pad(v, (0, n - v.shape[0])).reshape(1, n)


def _mixer_c(x, g, w_down, q_a_norm, kv_a_norm, w_q_up, w_kv_up, q_norm, k_norm, w_o, tables, ffn):
    cos, sin, _ = tables
    half = C_ROPE // 2
    lat = C_Q_RANK + C_KV_RANK
    w_rope = w_down[:, lat:]
    w_rope_partner = jnp.concatenate([w_rope[:, half:], w_rope[:, :half]], axis=1)
    at_rope_lanes = lambda w: jnp.pad(w, ((0, 0), (C_NOPE, LANES - C_DQK)))
    wd = jnp.concatenate([w_down[:, :lat], at_rope_lanes(w_rope), at_rope_lanes(w_rope_partner)], axis=1).astype(BF16)
    wq = w_q_up.reshape(C_Q_RANK, C_HEADS, C_DQK)
    wq = jnp.concatenate([wq, wq[:, :, C_NOPE:C_NOPE + half], jnp.zeros((C_Q_RANK, C_HEADS, half), F32)], axis=2)
    wq = wq.reshape(C_Q_RANK, C_HEADS * LANES).astype(BF16)
    wkv = w_kv_up.reshape(C_KV_RANK, C_HEADS, C_NOPE + C_V)
    wk = jnp.pad(wkv[:, :, :C_NOPE], ((0, 0), (0, 0), (0, LANES - C_NOPE))).reshape(C_KV_RANK, C_HEADS * LANES)
    wv = wkv[:, :, C_NOPE:].reshape(C_KV_RANK, C_HEADS * C_V)
    wkv = jnp.concatenate([wk, wv], axis=1).astype(BF16)
    qn = _pad_lanes(jnp.concatenate([q_norm, q_norm[C_NOPE:C_NOPE + half]]))
    kn_rope = k_norm[C_NOPE:]
    knp = _pad_lanes(jnp.concatenate([jnp.zeros((C_NOPE,), F32), kn_rope[half:], kn_rope[:half]]))
    q, k, v = _c_proj(x, g.reshape(1, -1), wd, q_a_norm.reshape(1, -1), kv_a_norm.reshape(1, -1), wq, wkv,
                      qn, _pad_lanes(k_norm), knp, _block_ones(LANES, C_DQK), cos, sin)
    return _ffn(_c_flash(q, k, v, x, w_o.astype(BF16)), ffn)


def kernel(x, positions, mix_norm, ffn_norm, a_w_qkv, a_q_norm, a_k_norm, a_sinks, a_w_o, b_w_in, b_conv_w, b_w_out,
           c_w_down, c_q_a_norm, c_kv_a_norm, c_w_q_up, c_w_kv_up, c_q_norm, c_k_norm, c_w_o, f_w_gate_up, f_w_down):
    batch, seq, d = x.shape
    assert batch == 1 and d == D_MODEL and seq % max(ROW_TILE, ROPE_TILE, MLA_TQ, MLA_TK) == 0
    depth = mix_norm.shape[0]
    pos_col = positions.reshape(seq, 1)
    pat_a = _rope_pattern(A_ROT_DIM, 0, A_HEAD_DIM)
    pat_c = _rope_pattern(C_ROPE, C_NOPE, LANES)
    tab_a = _rope_tables(pos_col, *pat_a) + (pat_a[1],)
    tab_c = _rope_tables(pos_col, *pat_c) + (pat_c[1],)
    xs = x.reshape(seq, d)
    for i in range(depth):
        kind, j = i % 3, i // 3
        ffn = (ffn_norm[i].reshape(1, -1), f_w_gate_up[i].astype(BF16), f_w_down[i].astype(BF16))
        if kind == 0:
            xs = _mixer_a(xs, mix_norm[i], a_w_qkv[j], a_q_norm[j], a_k_norm[j], a_sinks[j], a_w_o[j], tab_a, ffn)
        elif kind == 1:
            xs = _mixer_b(xs, mix_norm[i], b_w_in[j], b_conv_w[j], b_w_out[j], ffn)
        else:
            xs = _mixer_c(xs, mix_norm[i], c_w_down[j], c_q_a_norm[j], c_kv_a_norm[j], c_w_q_up[j], c_w_kv_up[j],
                          c_q_norm[j], c_k_norm[j], c_w_o[j], tab_c, ffn)
    return xs.reshape(batch, seq, d)
```

```python
import math

import jax
import jax.numpy as jnp
import numpy as np
from jax import lax
from jax.experimental import pallas as pl
from jax.experimental.pallas import tpu as pltpu

F32 = jnp.float32
BF16 = jnp.bfloat16

D_MODEL = 1024
ROPE_THETA = 500000.0
EPS = 1e-6
A_HEADS, A_KV_HEADS, A_HEAD_DIM, A_ROT_DIM, A_WINDOW = 16, 4, 64, 16, 128
C_HEADS, C_NOPE, C_ROPE, C_V, C_Q_RANK, C_KV_RANK = 16, 64, 32, 64, 384, 256
C_DQK = C_NOPE + C_ROPE
LOG2E = math.log2(math.e)

LANES = 128
VMEM_LIMIT_BYTES = 56 * 2**20

ROW_TILE = 512
ROPE_TILE = 2048
FFN_CHUNK = 256
MLA_TQ = 512
MLA_TK = 512
MLA_KSPLIT = 2
FLASH_STEP_KINDS = ((False, False), (False, True), (True,))
SOFTMAX_ROWS = 64


def _params(sem, flags=None):
    return pltpu.CompilerParams(dimension_semantics=sem, vmem_limit_bytes=VMEM_LIMIT_BYTES, flags=flags)


def _const_spec(shape):
    return pl.BlockSpec(shape, lambda *_: (0,) * len(shape), pipeline_mode=pl.Buffered(1))


def _rms(x, g):
    ms = jnp.mean(x * x, axis=-1, keepdims=True)
    return x * lax.rsqrt(ms + EPS) * g


def _dot(a, b):
    return jnp.dot(a, b, preferred_element_type=F32)


def _dot_nt(a, b):
    return lax.dot_general(a, b, (((1,), (1,)), ((), ())), preferred_element_type=F32)


def _lane_lo():
    return lax.broadcasted_iota(jnp.int32, (1, LANES), 1) < (LANES // 2)


def _rope_apply(x, cos, sin, first_half, half):
    fwd = pltpu.roll(x, LANES - half, 1)
    bwd = pltpu.roll(x, half, 1)
    return x * cos + jnp.where(first_half, fwd, bwd) * sin


def _rope_kernel(pos_ref, invf_ref, sign_ref, cos_ref, sin_ref):
    ang = pos_ref[...].astype(F32) * invf_ref[...]
    cos_ref[...] = jnp.cos(ang)
    sin_ref[...] = jnp.sin(ang) * sign_ref[...]


def _rope_tables(pos_col, invf, sign):
    s = pos_col.shape[0]
    row = pl.BlockSpec((1, LANES), lambda i: (0, 0))
    out = pl.BlockSpec((ROPE_TILE, LANES), lambda i: (i, 0))
    return pl.pallas_call(
        _rope_kernel,
        out_shape=(jax.ShapeDtypeStruct((s, LANES), F32),) * 2,
        grid=(s // ROPE_TILE,),
        in_specs=[pl.BlockSpec((ROPE_TILE, 1), lambda i: (i, 0)), row, row],
        out_specs=(out, out),
        compiler_params=_params(("parallel",)),
        name="rope_tables",
    )(pos_col, invf, sign)


def _rope_pattern(rot_dim, offset, period):
    half = rot_dim // 2
    inv_freq = ROPE_THETA ** (-jnp.arange(0, rot_dim, 2, dtype=F32) / rot_dim)
    d = np.arange(LANES) % period - offset
    rotary = (d >= 0) & (d < rot_dim)
    idx = np.where(rotary, d % half, 0)
    invf = jnp.where(jnp.asarray(rotary), inv_freq[idx], 0.0)
    sign = np.where(rotary, np.where(d < half, -1.0, 1.0), 0.0)
    return invf.reshape(1, LANES).astype(F32), jnp.asarray(sign, F32).reshape(1, LANES)


def _ffn_tile(x, g_ref, wgu_ref, wd_ref):
    h = _rms(x, g_ref[...]).astype(BF16)
    d_ff = wd_ref.shape[0]
    acc = x
    for c in range(d_ff // FFN_CHUNK):
        c0 = c * FFN_CHUNK
        gate = _dot(h, wgu_ref[:, c0:c0 + FFN_CHUNK])
        up = _dot(h, wgu_ref[:, d_ff + c0:d_ff + c0 + FFN_CHUNK])
        act = (gate * jax.nn.sigmoid(gate) * up).astype(BF16)
        acc = acc + _dot(act, wd_ref[c0:c0 + FFN_CHUNK, :])
    return acc


def _ffn_specs(ffn):
    g, wgu, wd = ffn
    return [_const_spec(g.shape), _const_spec(wgu.shape), _const_spec(wd.shape)]


def _ffn_kernel(x_ref, g_ref, wgu_ref, wd_ref, o_ref):
    o_ref[...] = _ffn_tile(x_ref[...], g_ref, wgu_ref, wd_ref)


def _ffn(x, ffn):
    s = x.shape[0]
    row = pl.BlockSpec((ROW_TILE, D_MODEL), lambda i: (i, 0))
    return pl.pallas_call(
        _ffn_kernel,
        out_shape=jax.ShapeDtypeStruct((s, D_MODEL), F32),
        grid=(s // ROW_TILE,),
        in_specs=[row] + _ffn_specs(ffn),
        out_specs=row,
        compiler_params=_params(("parallel",)),
        name="swiglu_ffn",
    )(x, *ffn)


def _block_ones(block, valid):
    r = np.arange(2 * LANES)
    same = (r[:, None] // block) == (r[None, :] // block)
    return jnp.asarray(same & ((r % block) < valid)[:, None], BF16)


def _a_proj_kernel(x_ref, g_ref, w_ref, gain_ref, ones_ref, cos_ref, sin_ref, sign_ref, q_ref, k_ref, v_ref, raw_ref):
    @pl.when(pl.program_id(0) == 0)
    def _():
        raw_ref[...] = jnp.zeros_like(raw_ref)

    lo = _lane_lo()
    nq = A_HEADS * A_HEAD_DIM
    nkv = A_KV_HEADS * A_HEAD_DIM
    cur = slice(0, ROW_TILE)
    qkv_cols = lambda c0, n: raw_ref[:, c0:c0 + n]
    first = sign_ref[...] < 0.0
    cos, sin = cos_ref[...], sin_ref[...]
    q_scale = A_HEAD_DIM ** -0.5 * LOG2E
    half_lanes = LANES // 2

    def split(col):
        a0 = jnp.where(lo, col, 0.0)
        b1 = jnp.where(lo, 0.0, col)
        return a0, pltpu.roll(a0, half_lanes, 1), pltpu.roll(b1, half_lanes, 1), b1

    for pr in range((nq + nkv) // (2 * LANES)):
        blk4 = qkv_cols(pr * 2 * LANES, 2 * LANES)
        ss = _dot((blk4 * blk4).astype(BF16), ones_ref[...])
        y = blk4 * lax.rsqrt(ss * (1.0 / A_HEAD_DIM) + EPS) * gain_ref[:, pr * 2 * LANES:(pr + 1) * 2 * LANES]
        for c in range(2):
            col = _rope_apply(y[:, c * LANES:(c + 1) * LANES], cos, sin, first, A_ROT_DIM // 2)
            cidx = 2 * pr + c
            if cidx < nq // LANES:
                q_ref[:, cidx * LANES:(cidx + 1) * LANES] = (col * q_scale).astype(BF16)
            else:
                kc = cidx - nq // LANES
                for t, part in enumerate(split(col)):
                    k_ref[cur, (4 * kc + t) * LANES:(4 * kc + t + 1) * LANES] = part.astype(BF16)
    vals = qkv_cols(nq + nkv, nkv)
    for c in range(nkv // LANES):
        a0, a1, b0, b1 = split(vals[:, c * LANES:(c + 1) * LANES])
        v_ref[cur, (2 * c) * LANES:(2 * c + 1) * LANES] = (a0 + a1).astype(BF16)
        v_ref[cur, (2 * c + 1) * LANES:(2 * c + 2) * LANES] = (b0 + b1).astype(BF16)
    raw_ref[...] = _dot(_rms(x_ref[...], g_ref[...]).astype(BF16), w_ref[...])


def _a_attn_kernel(sink_ref, q_ref, kp_ref, kc_ref, vp_ref, vc_ref, x_ref, wo_ref, gf_ref, wgu_ref, wd_ref, o_ref,
                   oacc_ref):
    i = pl.program_id(0)
    blk = A_WINDOW
    lo = _lane_lo()
    kcat = jnp.concatenate([kp_ref[...], kc_ref[...]], axis=0)
    vcat = jnp.concatenate([vp_ref[...], vc_ref[...]], axis=0)
    qi = lax.broadcasted_iota(jnp.int32, (2 * blk, 2 * blk), 0) & (blk - 1)
    kj = lax.broadcasted_iota(jnp.int32, (2 * blk, 2 * blk), 1)
    first_key = jnp.where(i > 0, 0, blk)
    ones = jnp.ones((2 * blk, LANES), BF16)
    for b in range(ROW_TILE // blk):
        valid = (kj > qi) & (kj <= qi + blk)
        if b == 0:
            valid = valid & (kj >= first_key)
        rows = slice(b * blk, (b + 1) * blk)
        win = slice(b * blk, (b + 2) * blk)
        for g in range(A_KV_HEADS):
            qs = jnp.concatenate([q_ref[rows, (2 * g) * LANES:(2 * g + 1) * LANES],
                                  q_ref[rows, (2 * g + 1) * LANES:(2 * g + 2) * LANES]], axis=0)
            vd = jnp.concatenate([vcat[win, g * LANES:(g + 1) * LANES], ones], axis=1)
            outs = []
            for half in range(2):
                kd = kcat[win, (2 * g + half) * LANES:(2 * g + half + 1) * LANES]
                sink = jnp.concatenate([jnp.full((blk, LANES), sink_ref[4 * g + half] * LOG2E, F32),
                                        jnp.full((blk, LANES), sink_ref[4 * g + 2 + half] * LOG2E, F32)], axis=0)
                s = jnp.where(valid, _dot_nt(qs, kd), -jnp.inf)
                s0, s1 = s[:, :LANES], s[:, LANES:]
                m = jnp.maximum(jnp.max(jnp.maximum(s0, s1), axis=-1, keepdims=True), sink)
                p = jnp.concatenate([jnp.exp2(s0 - m), jnp.exp2(s1 - m)], axis=1).astype(BF16)
                ol = _dot(p, vd)
                outs.append(ol[:, :LANES] / (ol[:, LANES:] + jnp.exp2(sink - m)))
            pair = jnp.where(lo, outs[0], outs[1])
            oacc_ref[rows, (2 * g) * LANES:(2 * g + 1) * LANES] = pair[:blk].astype(BF16)
            oacc_ref[rows, (2 * g + 1) * LANES:(2 * g + 2) * LANES] = pair[blk:].astype(BF16)

    o_ref[...] = _ffn_tile(x_ref[...] + _dot(oacc_ref[...], wo_ref[...]), gf_ref, wgu_ref, wd_ref)


def _mixer_a(x, g, w_qkv, q_norm, k_norm, sinks, w_o, tables, ffn):
    cos, sin, sign = tables
    s = x.shape[0]
    nq, nk, nv = A_HEADS * A_HEAD_DIM, 2 * A_KV_HEADS * LANES, A_KV_HEADS * LANES
    gain = jnp.concatenate([jnp.tile(q_norm, A_HEADS), jnp.tile(k_norm, A_KV_HEADS)]).reshape(1, -1)
    ones = _block_ones(A_HEAD_DIM, A_HEAD_DIM)
    w, wo = w_qkv.astype(BF16), w_o.astype(BF16)
    row = lambda n: pl.BlockSpec((ROW_TILE, n), lambda i: (i, 0))
    tiles = s // ROW_TILE
    ahead = lambda n: pl.BlockSpec((ROW_TILE, n), lambda i: (jnp.minimum(i, tiles - 1), 0))
    behind = lambda n: pl.BlockSpec((ROW_TILE, n), lambda i: (jnp.maximum(i - 1, 0), 0))
    q, k, v = pl.pallas_call(
        _a_proj_kernel,
        out_shape=(jax.ShapeDtypeStruct((s, nq), BF16), jax.ShapeDtypeStruct((s, nk), BF16),
                   jax.ShapeDtypeStruct((s, nv), BF16)),
        grid=(tiles + 1,),
        in_specs=[ahead(D_MODEL), _const_spec((1, D_MODEL)), _const_spec(w.shape), _const_spec(gain.shape),
                  _const_spec(ones.shape), behind(LANES), behind(LANES), _const_spec((1, LANES))],
        out_specs=(behind(nq), behind(nk), behind(nv)),
        scratch_shapes=[pltpu.VMEM((ROW_TILE, w.shape[1]), F32)],
        compiler_params=_params(("arbitrary",)),
        name="swa_qkv_proj",
    )(x, g.reshape(1, -1), w, gain, ones, cos, sin, sign)
    per = ROW_TILE // A_WINDOW
    prev = lambda n: pl.BlockSpec((A_WINDOW, n), lambda i: (jnp.maximum(i * per - 1, 0), 0))
    return pl.pallas_call(
        _a_attn_kernel,
        out_shape=jax.ShapeDtypeStruct((s, D_MODEL), F32),
        grid=(s // ROW_TILE,),
        in_specs=[pl.BlockSpec(memory_space=pltpu.SMEM), row(nq), prev(nk), row(nk), prev(nv), row(nv), row(D_MODEL),
                  _const_spec(wo.shape)] + _ffn_specs(ffn),
        out_specs=row(D_MODEL),
        scratch_shapes=[pltpu.VMEM((ROW_TILE, nq), BF16)],
        compiler_params=_params(("parallel",)),
        name="swa_attention_layer",
    )(sinks, q, k, k, v, v, x, wo, *ffn)


def _b_kernel(x_ref, g_ref, win_ref, cw_ref, wout_ref, gf_ref, wgu_ref, wd_ref, o_ref, carry_ref):
    @pl.when(pl.program_id(0) == 0)
    def _():
        carry_ref[...] = jnp.zeros_like(carry_ref)

    x = x_ref[...]
    h = _rms(x, g_ref[...]).astype(BF16)
    bcu = _dot(h, win_ref[...])
    z = bcu[:, D_MODEL:2 * D_MODEL] * bcu[:, 2 * D_MODEL:]
    r = lax.broadcasted_iota(jnp.int32, (ROW_TILE, 1), 0)
    z1 = jnp.where(r == 0, carry_ref[7:8, :], pltpu.roll(z, 1, 0))
    z2 = jnp.where(r == 0, carry_ref[6:7, :], jnp.where(r == 1, carry_ref[7:8, :], pltpu.roll(z, 2, 0)))
    y = cw_ref[0:1, :] * z2 + cw_ref[1:2, :] * z1 + cw_ref[2:3, :] * z
    carry_ref[...] = z[ROW_TILE - 8:]
    o_ref[...] = _ffn_tile(x + _dot((bcu[:, :D_MODEL] * y).astype(BF16), wout_ref[...]), gf_ref, wgu_ref, wd_ref)


def _mixer_b(x, g, w_in, conv_w, w_out, ffn):
    s = x.shape[0]
    row = pl.BlockSpec((ROW_TILE, D_MODEL), lambda i: (i, 0))
    return pl.pallas_call(
        _b_kernel,
        out_shape=jax.ShapeDtypeStruct((s, D_MODEL), F32),
        grid=(s // ROW_TILE,),
        in_specs=[row, _const_spec((1, D_MODEL)), _const_spec(w_in.shape), _const_spec(conv_w.shape),
                  _const_spec(w_out.shape)] + _ffn_specs(ffn),
        out_specs=row,
        scratch_shapes=[pltpu.VMEM((8, D_MODEL), F32)],
        compiler_params=_params(("arbitrary",)),
        name="short_conv_layer",
    )(x, g.reshape(1, -1), w_in.astype(BF16), conv_w, w_out.astype(BF16), *ffn)


def _c_proj_kernel(x_ref, g_ref, wd_ref, qan_ref, kvan_ref, wq_ref, wkv_ref, qn_ref, kn_ref, knp_ref, ones_ref,
                   cos_ref, sin_ref, q_ref, k_ref, v_ref):
    h = _rms(x_ref[...], g_ref[...]).astype(BF16)
    d = _dot(h, wd_ref[...])
    cq = _rms(d[:, :C_Q_RANK], qan_ref[...]).astype(BF16)
    lat = C_Q_RANK + C_KV_RANK
    ckv = _rms(d[:, C_Q_RANK:lat], kvan_ref[...]).astype(BF16)
    kr, krp = d[:, lat:lat + LANES], d[:, lat + LANES:]
    cos, sin = cos_ref[...], sin_ref[...]
    k_rot = (kr * kn_ref[...]) * cos + (krp * knp_ref[...]) * sin
    ss_rope = jnp.sum(kr * kr, axis=-1, keepdims=True)
    q_scale = C_DQK ** -0.5 * LOG2E
    qn2 = jnp.concatenate([qn_ref[...]] * 2, axis=1)
    for j in range(C_HEADS // 2):
        if j % 2 == 0:
            cols4 = slice(2 * j * LANES, (2 * j + 4) * LANES)
            qb4 = _dot(cq, wq_ref[:, cols4])
            kb4 = _dot(ckv, wkv_ref[:, cols4])
        half4 = slice((j % 2) * 2 * LANES, (j % 2 + 1) * 2 * LANES)
        qb, kb = qb4[:, half4], kb4[:, half4]
        ssq = _dot((qb * qb).astype(BF16), ones_ref[...])
        ssk = _dot((kb * kb).astype(BF16), ones_ref[...]) + ss_rope
        y = qb * lax.rsqrt(ssq * (1.0 / C_DQK) + EPS) * qn2
        rk = lax.rsqrt(ssk * (1.0 / C_DQK) + EPS)
        for c in range(2):
            lanes = slice(c * LANES, (c + 1) * LANES)
            yc = y[:, lanes]
            yc = yc * cos + pltpu.roll(yc, LANES - C_ROPE // 2, 1) * sin
            q_ref[2 * j + c] = (yc * q_scale).astype(BF16)
            k_ref[2 * j + c] = (rk[:, lanes] * (kb[:, lanes] * kn_ref[...] + k_rot)).astype(BF16)
    v_all = _dot(ckv, wkv_ref[:, C_HEADS * LANES:])
    for j in range(C_HEADS // 2):
        v_ref[j] = v_all[:, j * LANES:(j + 1) * LANES].astype(BF16)


def _c_proj(x, g, wd, qan, kvan, wq, wkv, qn, kn, knp, ones, cos, sin):
    s = x.shape[0]
    row = lambda n: pl.BlockSpec((ROW_TILE, n), lambda i: (i, 0))
    heads = lambda n: pl.BlockSpec((n, ROW_TILE, LANES), lambda i: (0, i, 0))
    return pl.pallas_call(
        _c_proj_kernel,
        out_shape=(jax.ShapeDtypeStruct((C_HEADS, s, LANES), BF16), jax.ShapeDtypeStruct((C_HEADS, s, LANES), BF16),
                   jax.ShapeDtypeStruct((C_HEADS // 2, s, LANES), BF16)),
        grid=(s // ROW_TILE,),
        in_specs=[row(D_MODEL), _const_spec((1, D_MODEL)), _const_spec(wd.shape), _const_spec(qan.shape),
                  _const_spec(kvan.shape), _const_spec(wq.shape), _const_spec(wkv.shape), _const_spec((1, LANES)),
                  _const_spec((1, LANES)), _const_spec((1, LANES)), _const_spec(ones.shape), row(LANES), row(LANES)],
        out_specs=(heads(C_HEADS), heads(C_HEADS), heads(C_HEADS // 2)),
        compiler_params=_params(("parallel",)),
        name="mla_proj",
    )(x, g, wd, qan, kvan, wq, wkv, qn, kn, knp, ones, cos, sin)


def _c_flash_kernel(qi_ref, ki_ref, kind_ref, last_ref, q_ref, k_ref, v_ref, x_ref, wo_ref, o_ref,
                    m_ref, l_ref, acc_ref, s0_ref, s1_ref, p0_ref, p1_ref, a0_ref, a1_ref):
    step = pl.program_id(0)
    qi, ki = qi_ref[step], ki_ref[step]
    lo = _lane_lo()
    s_bufs, p_bufs, a_bufs = (s0_ref, s1_ref), (p0_ref, p1_ref), (a0_ref, a1_ref)

    @pl.when(ki == 0)
    def _():
        m_ref[...] = jnp.full_like(m_ref, -jnp.inf)
        l_ref[...] = jnp.zeros_like(l_ref)
        acc_ref[...] = jnp.zeros_like(acc_ref)

    def sweep(sub_blocks):
        items = [(kb, h) for kb in range(len(sub_blocks)) for h in range(C_HEADS)]
        keys = lambda kb: slice(kb * MLA_TK, (kb + 1) * MLA_TK)
        causal = {}
        for kb, masked in enumerate(sub_blocks):
            if masked:
                qpos = qi * MLA_TQ + lax.broadcasted_iota(jnp.int32, (MLA_TQ, MLA_TK), 0)
                kpos = (ki * MLA_KSPLIT + kb) * MLA_TK + lax.broadcasted_iota(jnp.int32, (MLA_TQ, MLA_TK), 1)
                causal[kb] = kpos <= qpos

        def scores(item, slot):
            kb, h = item
            s = _dot_nt(q_ref[h], k_ref[h, keys(kb), :])
            if kb in causal:
                s = jnp.where(causal[kb], s, -jnp.inf)
            s_bufs[slot][...] = s

        def softmax(item, slot):
            h = item[1]
            s_ref, p_ref = s_bufs[slot], p_bufs[slot]
            for rb in range(MLA_TQ // SOFTMAX_ROWS):
                rows = slice(rb * SOFTMAX_ROWS, (rb + 1) * SOFTMAX_ROWS)
                chunks = [slice(c * LANES, (c + 1) * LANES) for c in range(MLA_TK // LANES)]
                part = s_ref[rows, chunks[0]]
                for cols in chunks[1:]:
                    part = jnp.maximum(part, s_ref[rows, cols])
                m_prev = m_ref[h, rows, :]
                m_new = jnp.maximum(m_prev, jnp.max(part, axis=-1, keepdims=True))
                alpha = jnp.exp2(m_prev - m_new)
                lsum = alpha * l_ref[h, rows, :]
                for cols in chunks:
                    p = jnp.exp2(s_ref[rows, cols] - m_new)
                    lsum = lsum + p
                    p_ref[rows, cols] = p.astype(BF16)
                l_ref[h, rows, :] = lsum
                m_ref[h, rows, :] = m_new
                a_bufs[slot][rows, :] = alpha

        def pv(item, slot):
            kb, h = item
            j = h // 2
            keep = lo if h % 2 == 0 else jnp.logical_not(lo)
            acc = acc_ref[j]
            pv_h = _dot(p_bufs[slot][...], v_ref[j, keys(kb), :])
            acc_ref[j] = jnp.where(keep, acc * a_bufs[slot][...] + pv_h, acc)

        for t in range(len(items) + 2):
            if t < len(items):
                scores(items[t], t % 2)
            if 1 <= t <= len(items):
                softmax(items[t - 1], (t - 1) % 2)
            if t >= 2:
                pv(items[t - 2], t % 2)

    for kind, sub_blocks in enumerate(FLASH_STEP_KINDS):
        @pl.when(kind_ref[step] == kind)
        def _():
            sweep(sub_blocks)

    @pl.when(last_ref[step] == 1)
    def _():
        cols = []
        for j in range(C_HEADS // 2):
            l_even = jnp.sum(l_ref[2 * j], axis=-1, keepdims=True)
            l_odd = jnp.sum(l_ref[2 * j + 1], axis=-1, keepdims=True)
            cols.append((acc_ref[j] / jnp.where(lo, l_even, l_odd)).astype(BF16))
        o_ref[...] = x_ref[...] + _dot(jnp.concatenate(cols, axis=1), wo_ref[...])


def _flash_schedule(s):
    assert MLA_TQ == MLA_TK
    qi, ki, kind, last = [], [], [], []
    for a in range(s // MLA_TQ):
        nk = a // MLA_KSPLIT + 1
        for b in range(nk):
            subs = [b * MLA_KSPLIT + kb for kb in range(MLA_KSPLIT) if b * MLA_KSPLIT + kb <= a]
            qi.append(a)
            ki.append(b)
            kind.append(FLASH_STEP_KINDS.index(tuple(g == a for g in subs)))
            last.append(int(b == nk - 1))
    return tuple(jnp.asarray(v, jnp.int32) for v in (qi, ki, kind, last))


def _c_flash(q, k, v, x, wo):
    s = x.shape[0]
    qi, ki, kind, last = _flash_schedule(s)
    key_rows = MLA_TK * MLA_KSPLIT
    grid_spec = pltpu.PrefetchScalarGridSpec(
        num_scalar_prefetch=4,
        grid=(int(qi.shape[0]),),
        in_specs=[
            pl.BlockSpec((C_HEADS, MLA_TQ, LANES), lambda t, qi, ki, kind, last: (0, qi[t], 0)),
            pl.BlockSpec((C_HEADS, key_rows, LANES), lambda t, qi, ki, kind, last: (0, ki[t], 0)),
            pl.BlockSpec((C_HEADS // 2, key_rows, LANES), lambda t, qi, ki, kind, last: (0, ki[t], 0)),
            pl.BlockSpec((MLA_TQ, D_MODEL), lambda t, qi, ki, kind, last: (qi[t], 0)),
            pl.BlockSpec(wo.shape, lambda t, qi, ki, kind, last: (0, 0), pipeline_mode=pl.Buffered(1)),
        ],
        out_specs=pl.BlockSpec((MLA_TQ, D_MODEL), lambda t, qi, ki, kind, last: (qi[t], 0)),
        scratch_shapes=[pltpu.VMEM((C_HEADS, MLA_TQ, LANES), F32), pltpu.VMEM((C_HEADS, MLA_TQ, LANES), F32),
                        pltpu.VMEM((C_HEADS // 2, MLA_TQ, LANES), F32),
                        pltpu.VMEM((MLA_TQ, MLA_TK), F32), pltpu.VMEM((MLA_TQ, MLA_TK), F32),
                        pltpu.VMEM((MLA_TQ, MLA_TK), BF16), pltpu.VMEM((MLA_TQ, MLA_TK), BF16),
                        pltpu.VMEM((MLA_TQ, LANES), F32), pltpu.VMEM((MLA_TQ, LANES), F32)],
    )
    return pl.pallas_call(
        _c_flash_kernel,
        out_shape=jax.ShapeDtypeStruct((s, D_MODEL), F32),
        grid_spec=grid_spec,
        compiler_params=_params(("arbitrary",)),
        name="mla_flash",
    )(qi, ki, kind, last, q, k, v, x, wo)


def _pad_lanes(v, n=LANES):
    return jnp.pad(v, (0, n - v.shape[0])).reshape(1, n)


def _mixer_c(x, g, w_down, q_a_norm, kv_a_norm, w_q_up, w_kv_up, q_norm, k_norm, w_o, tables, ffn):
    cos, sin, _ = tables
    half = C_ROPE // 2
    lat = C_Q_RANK + C_KV_RANK
    w_rope = w_down[:, lat:]
    w_rope_partner = jnp.concatenate([w_rope[:, half:], w_rope[:, :half]], axis=1)
    at_rope_lanes = lambda w: jnp.pad(w, ((0, 0), (C_NOPE, LANES - C_DQK)))
    wd = jnp.concatenate([w_down[:, :lat], at_rope_lanes(w_rope), at_rope_lanes(w_rope_partner)], axis=1).astype(BF16)
    wq = w_q_up.reshape(C_Q_RANK, C_HEADS, C_DQK)
    wq = jnp.concatenate([wq, wq[:, :, C_NOPE:C_NOPE + half], jnp.zeros((C_Q_RANK, C_HEADS, half), F32)], axis=2)
    wq = wq.reshape(C_Q_RANK, C_HEADS * LANES).astype(BF16)
    wkv = w_kv_up.reshape(C_KV_RANK, C_HEADS, C_NOPE + C_V)
    wk = jnp.pad(wkv[:, :, :C_NOPE], ((0, 0), (0, 0), (0, LANES - C_NOPE))).reshape(C_KV_RANK, C_HEADS * LANES)
    wv = wkv[:, :, C_NOPE:].reshape(C_KV_RANK, C_HEADS * C_V)
    wkv = jnp.concatenate([wk, wv], axis=1).astype(BF16)
    qn = _pad_lanes(jnp.concatenate([q_norm, q_norm[C_NOPE:C_NOPE + half]]))
    kn_rope = k_norm[C_NOPE:]
    knp = _pad_lanes(jnp.concatenate([jnp.zeros((C_NOPE,), F32), kn_rope[half:], kn_rope[:half]]))
    q, k, v = _c_proj(x, g.reshape(1, -1), wd, q_a_norm.reshape(1, -1), kv_a_norm.reshape(1, -1), wq, wkv,
                      qn, _pad_lanes(k_norm), knp, _block_ones(LANES, C_DQK), cos, sin)
    return _ffn(_c_flash(q, k, v, x, w_o.astype(BF16)), ffn)


def kernel(x, positions, mix_norm, ffn_norm, a_w_qkv, a_q_norm, a_k_norm, a_sinks, a_w_o, b_w_in, b_conv_w, b_w_out,
           c_w_down, c_q_a_norm, c_kv_a_norm, c_w_q_up, c_w_kv_up, c_q_norm, c_k_norm, c_w_o, f_w_gate_up, f_w_down):
    batch, seq, d = x.shape
    assert batch == 1 and d == D_MODEL and seq % max(ROW_TILE, ROPE_TILE, MLA_TQ, MLA_TK * MLA_KSPLIT) == 0
    depth = mix_norm.shape[0]
    pos_col = positions.reshape(seq, 1)
    pat_a = _rope_pattern(A_ROT_DIM, 0, A_HEAD_DIM)
    pat_c = _rope_pattern(C_ROPE, C_NOPE, LANES)
    tab_a = _rope_tables(pos_col, *pat_a) + (pat_a[1],)
    tab_c = _rope_tables(pos_col, *pat_c) + (pat_c[1],)
    xs = x.reshape(seq, d)
    for i in range(depth):
        kind, j = i % 3, i // 3
        ffn = (ffn_norm[i].reshape(1, -1), f_w_gate_up[i].astype(BF16), f_w_down[i].astype(BF16))
        if kind == 0:
            xs = _mixer_a(xs, mix_norm[i], a_w_qkv[j], a_q_norm[j], a_k_norm[j], a_sinks[j], a_w_o[j], tab_a, ffn)
        elif kind == 1:
            xs = _mixer_b(xs, mix_norm[i], b_w_in[j], b_conv_w[j], b_w_out[j], ffn)
        else:
            xs = _mixer_c(xs, mix_norm[i], c_w_down[j], c_q_a_norm[j], c_kv_a_norm[j], c_w_q_up[j], c_w_kv_up[j],
                          c_q_norm[j], c_k_norm[j], c_w_o[j], tab_c, ffn)
    return xs.reshape(batch, seq, d)
```

```python
import math

import jax
import jax.numpy as jnp
import numpy as np
from jax import lax
from jax.experimental import pallas as pl
from jax.experimental.pallas import tpu as pltpu

F32 = jnp.float32
BF16 = jnp.bfloat16

D_MODEL = 1024
ROPE_THETA = 500000.0
EPS = 1e-6
A_HEADS, A_KV_HEADS, A_HEAD_DIM, A_ROT_DIM, A_WINDOW = 16, 4, 64, 16, 128
C_HEADS, C_NOPE, C_ROPE, C_V, C_Q_RANK, C_KV_RANK = 16, 64, 32, 64, 384, 256
C_DQK = C_NOPE + C_ROPE
LOG2E = math.log2(math.e)

LANES = 128
SUBLANES = 8
VMEM_LIMIT_BYTES = 56 * 2**20

ROW_TILE = 512
ROPE_TILE = 2048
FFN_CHUNK = 256
MLA_TQ = 512
MLA_TK = 512
MLA_KSPLIT = 2
FLASH_STEP_KINDS = ((False, False), (False, True), (True,))
SOFTMAX_ROWS = 64


def _params(sem):
    return pltpu.CompilerParams(dimension_semantics=sem, vmem_limit_bytes=VMEM_LIMIT_BYTES)


def _const_spec(shape):
    return pl.BlockSpec(shape, lambda *_: (0,) * len(shape), pipeline_mode=pl.Buffered(1))


def _rms(x, g):
    ms = jnp.mean(x * x, axis=-1, keepdims=True)
    return x * lax.rsqrt(ms + EPS) * g


def _dot(a, b):
    return jnp.dot(a, b, preferred_element_type=F32)


def _dot_nt(a, b):
    return lax.dot_general(a, b, (((1,), (1,)), ((), ())), preferred_element_type=F32)


def _lane_lo():
    return lax.broadcasted_iota(jnp.int32, (1, LANES), 1) < (LANES // 2)


def _rope_apply(x, cos, sin, first_half, half):
    fwd = pltpu.roll(x, LANES - half, 1)
    bwd = pltpu.roll(x, half, 1)
    return x * cos + jnp.where(first_half, fwd, bwd) * sin


def _rope_kernel(pos_ref, invf_ref, sign_ref, cos_ref, sin_ref):
    ang = pos_ref[...].astype(F32) * invf_ref[...]
    cos_ref[...] = jnp.cos(ang)
    sin_ref[...] = jnp.sin(ang) * sign_ref[...]


def _rope_tables(pos_col, invf, sign):
    s = pos_col.shape[0]
    row = pl.BlockSpec((1, LANES), lambda i: (0, 0))
    out = pl.BlockSpec((ROPE_TILE, LANES), lambda i: (i, 0))
    return pl.pallas_call(
        _rope_kernel,
        out_shape=(jax.ShapeDtypeStruct((s, LANES), F32),) * 2,
        grid=(s // ROPE_TILE,),
        in_specs=[pl.BlockSpec((ROPE_TILE, 1), lambda i: (i, 0)), row, row],
        out_specs=(out, out),
        compiler_params=_params(("parallel",)),
        name="rope_tables",
    )(pos_col, invf, sign)


def _rope_pattern(rot_dim, offset, period):
    half = rot_dim // 2
    inv_freq = ROPE_THETA ** (-jnp.arange(0, rot_dim, 2, dtype=F32) / rot_dim)
    d = np.arange(LANES) % period - offset
    rotary = (d >= 0) & (d < rot_dim)
    idx = np.where(rotary, d % half, 0)
    invf = jnp.where(jnp.asarray(rotary), inv_freq[idx], 0.0)
    sign = np.where(rotary, np.where(d < half, -1.0, 1.0), 0.0)
    return invf.reshape(1, LANES).astype(F32), jnp.asarray(sign, F32).reshape(1, LANES)


def _ffn_tile(x, g_ref, wgu_ref, wd_ref):
    h = _rms(x, g_ref[...]).astype(BF16)
    d_ff = wd_ref.shape[0]
    acc = x
    for c in range(d_ff // FFN_CHUNK):
        c0 = c * FFN_CHUNK
        gate = _dot(h, wgu_ref[:, c0:c0 + FFN_CHUNK])
        up = _dot(h, wgu_ref[:, d_ff + c0:d_ff + c0 + FFN_CHUNK])
        act = (gate * jax.nn.sigmoid(gate) * up).astype(BF16)
        acc = acc + _dot(act, wd_ref[c0:c0 + FFN_CHUNK, :])
    return acc


def _ffn_specs(ffn):
    layer = ffn[0]
    pick = lambda a: pl.BlockSpec((None,) + a.shape[1:], lambda *_: (layer,) + (0,) * (a.ndim - 1),
                                  pipeline_mode=pl.Buffered(1))
    return [pick(a) for a in ffn[1:]]


def _ffn_kernel(x_ref, g_ref, wgu_ref, wd_ref, o_ref):
    o_ref[...] = _ffn_tile(x_ref[...], g_ref, wgu_ref, wd_ref)


def _ffn(x, ffn):
    s = x.shape[0]
    row = pl.BlockSpec((ROW_TILE, D_MODEL), lambda i: (i, 0))
    return pl.pallas_call(
        _ffn_kernel,
        out_shape=jax.ShapeDtypeStruct((s, D_MODEL), F32),
        grid=(s // ROW_TILE,),
        in_specs=[row] + _ffn_specs(ffn),
        out_specs=row,
        compiler_params=_params(("parallel",)),
        name="swiglu_ffn",
    )(x, *ffn[1:])


def _block_ones(block, valid):
    r = np.arange(2 * LANES)
    same = (r[:, None] // block) == (r[None, :] // block)
    return jnp.asarray(same & ((r % block) < valid)[:, None], BF16)


def _a_proj_kernel(x_ref, g_ref, w_ref, gain_ref, ones_ref, cos_ref, sin_ref, sign_ref, q_ref, k_ref, v_ref, raw_ref):
    @pl.when(pl.program_id(0) == 0)
    def _():
        raw_ref[...] = jnp.zeros_like(raw_ref)

    lo = _lane_lo()
    nq = A_HEADS * A_HEAD_DIM
    nkv = A_KV_HEADS * A_HEAD_DIM
    cur = slice(0, ROW_TILE)
    qkv_cols = lambda c0, n: raw_ref[:, c0:c0 + n]
    first = sign_ref[...] < 0.0
    cos, sin = cos_ref[...], sin_ref[...]
    q_scale = A_HEAD_DIM ** -0.5 * LOG2E
    half_lanes = LANES // 2

    def split(col):
        a0 = jnp.where(lo, col, 0.0)
        b1 = jnp.where(lo, 0.0, col)
        return a0, pltpu.roll(a0, half_lanes, 1), pltpu.roll(b1, half_lanes, 1), b1

    for pr in range((nq + nkv) // (2 * LANES)):
        blk4 = qkv_cols(pr * 2 * LANES, 2 * LANES)
        ss = _dot((blk4 * blk4).astype(BF16), ones_ref[...])
        y = blk4 * lax.rsqrt(ss * (1.0 / A_HEAD_DIM) + EPS) * gain_ref[:, pr * 2 * LANES:(pr + 1) * 2 * LANES]
        for c in range(2):
            col = _rope_apply(y[:, c * LANES:(c + 1) * LANES], cos, sin, first, A_ROT_DIM // 2)
            cidx = 2 * pr + c
            if cidx < nq // LANES:
                q_ref[:, cidx * LANES:(cidx + 1) * LANES] = (col * q_scale).astype(BF16)
            else:
                kc = cidx - nq // LANES
                for t, part in enumerate(split(col)):
                    k_ref[cur, (4 * kc + t) * LANES:(4 * kc + t + 1) * LANES] = part.astype(BF16)
    vals = qkv_cols(nq + nkv, nkv)
    for c in range(nkv // LANES):
        a0, a1, b0, b1 = split(vals[:, c * LANES:(c + 1) * LANES])
        v_ref[cur, (2 * c) * LANES:(2 * c + 1) * LANES] = (a0 + a1).astype(BF16)
        v_ref[cur, (2 * c + 1) * LANES:(2 * c + 2) * LANES] = (b0 + b1).astype(BF16)
    raw_ref[...] = _dot(_rms(x_ref[...], g_ref[...]).astype(BF16), w_ref[...])


def _a_attn_kernel(sink_ref, q_ref, kp_ref, kc_ref, vp_ref, vc_ref, x_ref, wo_ref, gf_ref, wgu_ref, wd_ref, o_ref,
                   oacc_ref):
    i = pl.program_id(0)
    blk = A_WINDOW
    lo = _lane_lo()
    kcat = jnp.concatenate([kp_ref[...], kc_ref[...]], axis=0)
    vcat = jnp.concatenate([vp_ref[...], vc_ref[...]], axis=0)
    qi = lax.broadcasted_iota(jnp.int32, (2 * blk, 2 * blk), 0) & (blk - 1)
    kj = lax.broadcasted_iota(jnp.int32, (2 * blk, 2 * blk), 1)
    first_key = jnp.where(i > 0, 0, blk)
    ones = jnp.ones((2 * blk, LANES), BF16)
    for b in range(ROW_TILE // blk):
        valid = (kj > qi) & (kj <= qi + blk)
        if b == 0:
            valid = valid & (kj >= first_key)
        rows = slice(b * blk, (b + 1) * blk)
        win = slice(b * blk, (b + 2) * blk)
        for g in range(A_KV_HEADS):
            qs = jnp.concatenate([q_ref[rows, (2 * g) * LANES:(2 * g + 1) * LANES],
                                  q_ref[rows, (2 * g + 1) * LANES:(2 * g + 2) * LANES]], axis=0)
            vd = jnp.concatenate([vcat[win, g * LANES:(g + 1) * LANES], ones], axis=1)
            outs = []
            for half in range(2):
                kd = kcat[win, (2 * g + half) * LANES:(2 * g + half + 1) * LANES]
                sink = jnp.concatenate([jnp.full((blk, LANES), sink_ref[4 * g + half] * LOG2E, F32),
                                        jnp.full((blk, LANES), sink_ref[4 * g + 2 + half] * LOG2E, F32)], axis=0)
                s = jnp.where(valid, _dot_nt(qs, kd), -jnp.inf)
                s0, s1 = s[:, :LANES], s[:, LANES:]
                m = jnp.maximum(jnp.max(jnp.maximum(s0, s1), axis=-1, keepdims=True), sink)
                p = jnp.concatenate([jnp.exp2(s0 - m), jnp.exp2(s1 - m)], axis=1).astype(BF16)
                ol = _dot(p, vd)
                outs.append(ol[:, :LANES] / (ol[:, LANES:] + jnp.exp2(sink - m)))
            pair = jnp.where(lo, outs[0], outs[1])
            oacc_ref[rows, (2 * g) * LANES:(2 * g + 1) * LANES] = pair[:blk].astype(BF16)
            oacc_ref[rows, (2 * g + 1) * LANES:(2 * g + 2) * LANES] = pair[blk:].astype(BF16)

    o_ref[...] = _ffn_tile(x_ref[...] + _dot(oacc_ref[...], wo_ref[...]), gf_ref, wgu_ref, wd_ref)


def _mixer_a(x, g, w_qkv, q_norm, k_norm, sinks, w_o, tables, ffn):
    cos, sin, sign = tables
    s = x.shape[0]
    nq, nk, nv = A_HEADS * A_HEAD_DIM, 2 * A_KV_HEADS * LANES, A_KV_HEADS * LANES
    gain = jnp.concatenate([jnp.tile(q_norm, A_HEADS), jnp.tile(k_norm, A_KV_HEADS)]).reshape(1, -1)
    ones = _block_ones(A_HEAD_DIM, A_HEAD_DIM)
    w, wo = w_qkv.astype(BF16), w_o.astype(BF16)
    row = lambda n: pl.BlockSpec((ROW_TILE, n), lambda i: (i, 0))
    tiles = s // ROW_TILE
    ahead = lambda n: pl.BlockSpec((ROW_TILE, n), lambda i: (jnp.minimum(i, tiles - 1), 0))
    behind = lambda n: pl.BlockSpec((ROW_TILE, n), lambda i: (jnp.maximum(i - 1, 0), 0))
    q, k, v = pl.pallas_call(
        _a_proj_kernel,
        out_shape=(jax.ShapeDtypeStruct((s, nq), BF16), jax.ShapeDtypeStruct((s, nk), BF16),
                   jax.ShapeDtypeStruct((s, nv), BF16)),
        grid=(tiles + 1,),
        in_specs=[ahead(D_MODEL), _const_spec((1, D_MODEL)), _const_spec(w.shape), _const_spec(gain.shape),
                  _const_spec(ones.shape), behind(LANES), behind(LANES), _const_spec((1, LANES))],
        out_specs=(behind(nq), behind(nk), behind(nv)),
        scratch_shapes=[pltpu.VMEM((ROW_TILE, w.shape[1]), F32)],
        compiler_params=_params(("arbitrary",)),
        name="swa_qkv_proj",
    )(x, g.reshape(1, -1), w, gain, ones, cos, sin, sign)
    per = ROW_TILE // A_WINDOW
    prev = lambda n: pl.BlockSpec((A_WINDOW, n), lambda i: (jnp.maximum(i * per - 1, 0), 0))
    return pl.pallas_call(
        _a_attn_kernel,
        out_shape=jax.ShapeDtypeStruct((s, D_MODEL), F32),
        grid=(s // ROW_TILE,),
        in_specs=[pl.BlockSpec(memory_space=pltpu.SMEM), row(nq), prev(nk), row(nk), prev(nv), row(nv), row(D_MODEL),
                  _const_spec(wo.shape)] + _ffn_specs(ffn),
        out_specs=row(D_MODEL),
        scratch_shapes=[pltpu.VMEM((ROW_TILE, nq), BF16)],
        compiler_params=_params(("parallel",)),
        name="swa_attention_layer",
    )(sinks, q, k, k, v, v, x, wo, *ffn[1:])


def _b_kernel(x_ref, g_ref, win_ref, cw_ref, wout_ref, gf_ref, wgu_ref, wd_ref, o_ref, carry_ref):
    @pl.when(pl.program_id(0) == 0)
    def _():
        carry_ref[...] = jnp.zeros_like(carry_ref)

    x = x_ref[...]
    h = _rms(x, g_ref[...]).astype(BF16)
    bcu = _dot(h, win_ref[...])
    z = bcu[:, D_MODEL:2 * D_MODEL] * bcu[:, 2 * D_MODEL:]
    r = lax.broadcasted_iota(jnp.int32, (ROW_TILE, 1), 0)
    prev1 = carry_ref[SUBLANES - 1:SUBLANES, :]
    prev2 = carry_ref[SUBLANES - 2:SUBLANES - 1, :]
    z1 = jnp.where(r == 0, prev1, pltpu.roll(z, 1, 0))
    z2 = jnp.where(r == 0, prev2, jnp.where(r == 1, prev1, pltpu.roll(z, 2, 0)))
    y = cw_ref[0:1, :] * z2 + cw_ref[1:2, :] * z1 + cw_ref[2:3, :] * z
    carry_ref[...] = z[ROW_TILE - SUBLANES:]
    o_ref[...] = _ffn_tile(x + _dot((bcu[:, :D_MODEL] * y).astype(BF16), wout_ref[...]), gf_ref, wgu_ref, wd_ref)


def _mixer_b(x, g, w_in, conv_w, w_out, ffn):
    s = x.shape[0]
    row = pl.BlockSpec((ROW_TILE, D_MODEL), lambda i: (i, 0))
    return pl.pallas_call(
        _b_kernel,
        out_shape=jax.ShapeDtypeStruct((s, D_MODEL), F32),
        grid=(s // ROW_TILE,),
        in_specs=[row, _const_spec((1, D_MODEL)), _const_spec(w_in.shape), _const_spec(conv_w.shape),
                  _const_spec(w_out.shape)] + _ffn_specs(ffn),
        out_specs=row,
        scratch_shapes=[pltpu.VMEM((SUBLANES, D_MODEL), F32)],
        compiler_params=_params(("arbitrary",)),
        name="short_conv_layer",
    )(x, g.reshape(1, -1), w_in.astype(BF16), conv_w, w_out.astype(BF16), *ffn[1:])


def _c_proj_kernel(x_ref, g_ref, wd_ref, qan_ref, kvan_ref, wq_ref, wkv_ref, qn_ref, kn_ref, knp_ref, ones_ref,
                   cos_ref, sin_ref, q_ref, k_ref, v_ref):
    h = _rms(x_ref[...], g_ref[...]).astype(BF16)
    d = _dot(h, wd_ref[...])
    cq = _rms(d[:, :C_Q_RANK], qan_ref[...]).astype(BF16)
    lat = C_Q_RANK + C_KV_RANK
    ckv = _rms(d[:, C_Q_RANK:lat], kvan_ref[...]).astype(BF16)
    kr, krp = d[:, lat:lat + LANES], d[:, lat + LANES:]
    cos, sin = cos_ref[...], sin_ref[...]
    k_rot = (kr * kn_ref[...]) * cos + (krp * knp_ref[...]) * sin
    ss_rope = jnp.sum(kr * kr, axis=-1, keepdims=True)
    q_scale = C_DQK ** -0.5 * LOG2E
    qn2 = jnp.concatenate([qn_ref[...]] * 2, axis=1)
    for j in range(C_HEADS // 2):
        if j % 2 == 0:
            cols4 = slice(2 * j * LANES, (2 * j + 4) * LANES)
            qb4 = _dot(cq, wq_ref[:, cols4])
            kb4 = _dot(ckv, wkv_ref[:, cols4])
        half4 = slice((j % 2) * 2 * LANES, (j % 2 + 1) * 2 * LANES)
        qb, kb = qb4[:, half4], kb4[:, half4]
        ssq = _dot((qb * qb).astype(BF16), ones_ref[...])
        ssk = _dot((kb * kb).astype(BF16), ones_ref[...]) + ss_rope
        y = qb * lax.rsqrt(ssq * (1.0 / C_DQK) + EPS) * qn2
        rk = lax.rsqrt(ssk * (1.0 / C_DQK) + EPS)
        for c in range(2):
            lanes = slice(c * LANES, (c + 1) * LANES)
            yc = y[:, lanes]
            yc = yc * cos + pltpu.roll(yc, LANES - C_ROPE // 2, 1) * sin
            q_ref[2 * j + c] = (yc * q_scale).astype(BF16)
            k_ref[2 * j + c] = (rk[:, lanes] * (kb[:, lanes] * kn_ref[...] + k_rot)).astype(BF16)
    v_all = _dot(ckv, wkv_ref[:, C_HEADS * LANES:])
    for j in range(C_HEADS // 2):
        v_ref[j] = v_all[:, j * LANES:(j + 1) * LANES].astype(BF16)


def _c_proj(x, g, wd, qan, kvan, wq, wkv, qn, kn, knp, ones, cos, sin):
    s = x.shape[0]
    row = lambda n: pl.BlockSpec((ROW_TILE, n), lambda i: (i, 0))
    heads = lambda n: pl.BlockSpec((n, ROW_TILE, LANES), lambda i: (0, i, 0))
    return pl.pallas_call(
        _c_proj_kernel,
        out_shape=(jax.ShapeDtypeStruct((C_HEADS, s, LANES), BF16), jax.ShapeDtypeStruct((C_HEADS, s, LANES), BF16),
                   jax.ShapeDtypeStruct((C_HEADS // 2, s, LANES), BF16)),
        grid=(s // ROW_TILE,),
        in_specs=[row(D_MODEL), _const_spec((1, D_MODEL)), _const_spec(wd.shape), _const_spec(qan.shape),
                  _const_spec(kvan.shape), _const_spec(wq.shape), _const_spec(wkv.shape), _const_spec((1, LANES)),
                  _const_spec((1, LANES)), _const_spec((1, LANES)), _const_spec(ones.shape), row(LANES), row(LANES)],
        out_specs=(heads(C_HEADS), heads(C_HEADS), heads(C_HEADS // 2)),
        compiler_params=_params(("parallel",)),
        name="mla_proj",
    )(x, g, wd, qan, kvan, wq, wkv, qn, kn, knp, ones, cos, sin)


def _c_flash_kernel(qi_ref, ki_ref, kind_ref, last_ref, q_ref, k_ref, v_ref, x_ref, wo_ref, o_ref,
                    m_ref, l_ref, acc_ref, s0_ref, s1_ref, p0_ref, p1_ref, a0_ref, a1_ref):
    step = pl.program_id(0)
    qi, ki = qi_ref[step], ki_ref[step]
    lo = _lane_lo()
    s_bufs, p_bufs, a_bufs = (s0_ref, s1_ref), (p0_ref, p1_ref), (a0_ref, a1_ref)

    @pl.when(ki == 0)
    def _():
        m_ref[...] = jnp.full_like(m_ref, -jnp.inf)
        l_ref[...] = jnp.zeros_like(l_ref)
        acc_ref[...] = jnp.zeros_like(acc_ref)

    def sweep(sub_blocks):
        items = [(kb, h) for kb in range(len(sub_blocks)) for h in range(C_HEADS)]
        keys = lambda kb: slice(kb * MLA_TK, (kb + 1) * MLA_TK)
        causal = {}
        for kb, masked in enumerate(sub_blocks):
            if masked:
                qpos = qi * MLA_TQ + lax.broadcasted_iota(jnp.int32, (MLA_TQ, MLA_TK), 0)
                kpos = (ki * MLA_KSPLIT + kb) * MLA_TK + lax.broadcasted_iota(jnp.int32, (MLA_TQ, MLA_TK), 1)
                causal[kb] = kpos <= qpos

        def scores(item, slot):
            kb, h = item
            s = _dot_nt(q_ref[h], k_ref[h, keys(kb), :])
            if kb in causal:
                s = jnp.where(causal[kb], s, -jnp.inf)
            s_bufs[slot][...] = s

        def softmax(item, slot):
            h = item[1]
            s_ref, p_ref = s_bufs[slot], p_bufs[slot]
            for rb in range(MLA_TQ // SOFTMAX_ROWS):
                rows = slice(rb * SOFTMAX_ROWS, (rb + 1) * SOFTMAX_ROWS)
                chunks = [slice(c * LANES, (c + 1) * LANES) for c in range(MLA_TK // LANES)]
                part = s_ref[rows, chunks[0]]
                for cols in chunks[1:]:
                    part = jnp.maximum(part, s_ref[rows, cols])
                m_prev = m_ref[h, rows, :]
                m_new = jnp.maximum(m_prev, jnp.max(part, axis=-1, keepdims=True))
                alpha = jnp.exp2(m_prev - m_new)
                lsum = alpha * l_ref[h, rows, :]
                for cols in chunks:
                    p = jnp.exp2(s_ref[rows, cols] - m_new)
                    lsum = lsum + p
                    p_ref[rows, cols] = p.astype(BF16)
                l_ref[h, rows, :] = lsum
                m_ref[h, rows, :] = m_new
                a_bufs[slot][rows, :] = alpha

        def pv(item, slot):
            kb, h = item
            j = h // 2
            keep = lo if h % 2 == 0 else jnp.logical_not(lo)
            acc = acc_ref[j]
            pv_h = _dot(p_bufs[slot][...], v_ref[j, keys(kb), :])
            acc_ref[j] = jnp.where(keep, acc * a_bufs[slot][...] + pv_h, acc)

        for t in range(len(items) + 2):
            if t < len(items):
                scores(items[t], t % 2)
            if 1 <= t <= len(items):
                softmax(items[t - 1], (t - 1) % 2)
            if t >= 2:
                pv(items[t - 2], t % 2)

    for kind, sub_blocks in enumerate(FLASH_STEP_KINDS):
        @pl.when(kind_ref[step] == kind)
        def _():
            sweep(sub_blocks)

    @pl.when(last_ref[step] == 1)
    def _():
        cols = []
        for j in range(C_HEADS // 2):
            l_even = jnp.sum(l_ref[2 * j], axis=-1, keepdims=True)
            l_odd = jnp.sum(l_ref[2 * j + 1], axis=-1, keepdims=True)
            cols.append((acc_ref[j] / jnp.where(lo, l_even, l_odd)).astype(BF16))
        o_ref[...] = x_ref[...] + _dot(jnp.concatenate(cols, axis=1), wo_ref[...])


def _flash_schedule(s):
    assert MLA_TQ == MLA_TK
    qi, ki, kind, last = [], [], [], []
    for a in range(s // MLA_TQ):
        nk = a // MLA_KSPLIT + 1
        for b in range(nk):
            subs = [b * MLA_KSPLIT + kb for kb in range(MLA_KSPLIT) if b * MLA_KSPLIT + kb <= a]
            qi.append(a)
            ki.append(b)
            kind.append(FLASH_STEP_KINDS.index(tuple(g == a for g in subs)))
            last.append(int(b == nk - 1))
    return tuple(jnp.asarray(v, jnp.int32) for v in (qi, ki, kind, last))


def _c_flash(q, k, v, x, wo):
    s = x.shape[0]
    qi, ki, kind, last = _flash_schedule(s)
    key_rows = MLA_TK * MLA_KSPLIT
    grid_spec = pltpu.PrefetchScalarGridSpec(
        num_scalar_prefetch=4,
        grid=(int(qi.shape[0]),),
        in_specs=[
            pl.BlockSpec((C_HEADS, MLA_TQ, LANES), lambda t, qi, ki, kind, last: (0, qi[t], 0)),
            pl.BlockSpec((C_HEADS, key_rows, LANES), lambda t, qi, ki, kind, last: (0, ki[t], 0)),
            pl.BlockSpec((C_HEADS // 2, key_rows, LANES), lambda t, qi, ki, kind, last: (0, ki[t], 0)),
            pl.BlockSpec((MLA_TQ, D_MODEL), lambda t, qi, ki, kind, last: (qi[t], 0)),
            pl.BlockSpec(wo.shape, lambda t, qi, ki, kind, last: (0, 0), pipeline_mode=pl.Buffered(1)),
        ],
        out_specs=pl.BlockSpec((MLA_TQ, D_MODEL), lambda t, qi, ki, kind, last: (qi[t], 0)),
        scratch_shapes=[pltpu.VMEM((C_HEADS, MLA_TQ, LANES), F32), pltpu.VMEM((C_HEADS, MLA_TQ, LANES), F32),
                        pltpu.VMEM((C_HEADS // 2, MLA_TQ, LANES), F32),
                        pltpu.VMEM((MLA_TQ, MLA_TK), F32), pltpu.VMEM((MLA_TQ, MLA_TK), F32),
                        pltpu.VMEM((MLA_TQ, MLA_TK), BF16), pltpu.VMEM((MLA_TQ, MLA_TK), BF16),
                        pltpu.VMEM((MLA_TQ, LANES), F32), pltpu.VMEM((MLA_TQ, LANES), F32)],
    )
    return pl.pallas_call(
        _c_flash_kernel,
        out_shape=jax.ShapeDtypeStruct((s, D_MODEL), F32),
        grid_spec=grid_spec,
        compiler_params=_params(("arbitrary",)),
        name="mla_flash",
    )(qi, ki, kind, last, q, k, v, x, wo)


def _pad_lanes(v, n=LANES):
    return jnp.pad(v, (0, n - v.shape[0])).reshape(1, n)


def _mixer_c(x, g, w_down, q_a_norm, kv_a_norm, w_q_up, w_kv_up, q_norm, k_norm, w_o, tables, ffn):
    cos, sin, _ = tables
    half = C_ROPE // 2
    lat = C_Q_RANK + C_KV_RANK
    w_rope = w_down[:, lat:]
    w_rope_partner = jnp.concatenate([w_rope[:, half:], w_rope[:, :half]], axis=1)
    at_rope_lanes = lambda w: jnp.pad(w, ((0, 0), (C_NOPE, LANES - C_DQK)))
    wd = jnp.concatenate([w_down[:, :lat], at_rope_lanes(w_rope), at_rope_lanes(w_rope_partner)], axis=1).astype(BF16)
    wq = w_q_up.reshape(C_Q_RANK, C_HEADS, C_DQK)
    wq = jnp.concatenate([wq, wq[:, :, C_NOPE:C_NOPE + half], jnp.zeros((C_Q_RANK, C_HEADS, half), F32)], axis=2)
    wq = wq.reshape(C_Q_RANK, C_HEADS * LANES).astype(BF16)
    wkv = w_kv_up.reshape(C_KV_RANK, C_HEADS, C_NOPE + C_V)
    wk = jnp.pad(wkv[:, :, :C_NOPE], ((0, 0), (0, 0), (0, LANES - C_NOPE))).reshape(C_KV_RANK, C_HEADS * LANES)
    wv = wkv[:, :, C_NOPE:].reshape(C_KV_RANK, C_HEADS * C_V)
    wkv = jnp.concatenate([wk, wv], axis=1).astype(BF16)
    qn = _pad_lanes(jnp.concatenate([q_norm, q_norm[C_NOPE:C_NOPE + half]]))
    kn_rope = k_norm[C_NOPE:]
    knp = _pad_lanes(jnp.concatenate([jnp.zeros((C_NOPE,), F32), kn_rope[half:], kn_rope[:half]]))
    q, k, v = _c_proj(x, g.reshape(1, -1), wd, q_a_norm.reshape(1, -1), kv_a_norm.reshape(1, -1), wq, wkv,
                      qn, _pad_lanes(k_norm), knp, _block_ones(LANES, C_DQK), cos, sin)
    return _ffn(_c_flash(q, k, v, x, w_o.astype(BF16)), ffn)


def kernel(x, positions, mix_norm, ffn_norm, a_w_qkv, a_q_norm, a_k_norm, a_sinks, a_w_o, b_w_in, b_conv_w, b_w_out,
           c_w_down, c_q_a_norm, c_kv_a_norm, c_w_q_up, c_w_kv_up, c_q_norm, c_k_norm, c_w_o, f_w_gate_up, f_w_down):
    batch, seq, d = x.shape
    assert batch == 1 and d == D_MODEL and seq % max(ROW_TILE, ROPE_TILE, MLA_TQ, MLA_TK * MLA_KSPLIT) == 0
    depth = mix_norm.shape[0]
    pos_col = positions.reshape(seq, 1)
    pat_a = _rope_pattern(A_ROT_DIM, 0, A_HEAD_DIM)
    pat_c = _rope_pattern(C_ROPE, C_NOPE, LANES)
    tab_a = _rope_tables(pos_col, *pat_a) + (pat_a[1],)
    tab_c = _rope_tables(pos_col, *pat_c) + (pat_c[1],)
    xs = x.reshape(seq, d)
    ffn_all = (ffn_norm.reshape(depth, 1, d), f_w_gate_up.astype(BF16), f_w_down.astype(BF16))
    for i in range(depth):
        kind, j = i % 3, i // 3
        ffn = (i,) + ffn_all
        if kind == 0:
            xs = _mixer_a(xs, mix_norm[i], a_w_qkv[j], a_q_norm[j], a_k_norm[j], a_sinks[j], a_w_o[j], tab_a, ffn)
        elif kind == 1:
            xs = _mixer_b(xs, mix_norm[i], b_w_in[j], b_conv_w[j], b_w_out[j], ffn)
        else:
            xs = _mixer_c(xs, mix_norm[i], c_w_down[j], c_q_a_norm[j], c_kv_a_norm[j], c_w_q_up[j], c_w_kv_up[j],
                          c_q_norm[j], c_k_norm[j], c_w_o[j], tab_c, ffn)
    return xs.reshape(batch, seq, d)
```

```python
import functools
import math

import jax
import jax.numpy as jnp
import numpy as np
from jax import lax
from jax.experimental import pallas as pl
from jax.experimental.pallas import tpu as pltpu

F32 = jnp.float32
BF16 = jnp.bfloat16

D_MODEL = 1024
ROPE_THETA = 500000.0
EPS = 1e-6
A_HEADS, A_KV_HEADS, A_HEAD_DIM, A_ROT_DIM, A_WINDOW = 16, 4, 64, 16, 128
C_HEADS, C_NOPE, C_ROPE, C_V, C_Q_RANK, C_KV_RANK = 16, 64, 32, 64, 384, 256
C_DQK = C_NOPE + C_ROPE
LOG2E = math.log2(math.e)

LANES = 128
SUBLANES = 8
VMEM_LIMIT_BYTES = 56 * 2**20

ROW_TILE = 512
ROPE_TILE = 2048
FFN_CHUNK = 256
MLA_TQ = 512
MLA_TK = 512
MLA_KSPLIT = 2
FLASH_STEP_KINDS = ((False, False), (False, True), (True,))
SOFTMAX_ROWS = 64


def _params(sem):
    return pltpu.CompilerParams(dimension_semantics=sem, vmem_limit_bytes=VMEM_LIMIT_BYTES)


def _const_spec(shape):
    return pl.BlockSpec(shape, lambda *_: (0,) * len(shape), pipeline_mode=pl.Buffered(1))


def _rms(x, g):
    ms = jnp.mean(x * x, axis=-1, keepdims=True)
    return x * lax.rsqrt(ms + EPS) * g


def _dot(a, b):
    return jnp.dot(a, b, preferred_element_type=F32)


def _dot_nt(a, b):
    return lax.dot_general(a, b, (((1,), (1,)), ((), ())), preferred_element_type=F32)


def _lane_lo():
    return lax.broadcasted_iota(jnp.int32, (1, LANES), 1) < (LANES // 2)


def _rope_apply(x, cos, sin, first_half, half):
    fwd = pltpu.roll(x, LANES - half, 1)
    bwd = pltpu.roll(x, half, 1)
    return x * cos + jnp.where(first_half, fwd, bwd) * sin


def _rope_kernel(groups, pos_ref, invf_ref, cos_ref, sin_ref):
    ang = invf_ref[...] * pos_ref[...].astype(F32)
    c, s = jnp.cos(ang), jnp.sin(ang)
    ones = jnp.ones((SUBLANES, ROPE_TILE), F32)
    zeros = jnp.zeros((SUBLANES, ROPE_TILE), F32)
    cos_rows, sin_rows = [], []
    for grp in groups:
        if grp is None:
            cos_rows.append(ones)
            sin_rows.append(zeros)
        else:
            r0, sign = grp
            cos_rows.append(c[r0:r0 + SUBLANES])
            sin_rows.append(s[r0:r0 + SUBLANES] if sign > 0 else -s[r0:r0 + SUBLANES])
    cos_ref[...] = jnp.concatenate(cos_rows, axis=0).T
    sin_ref[...] = jnp.concatenate(sin_rows, axis=0).T


def _rope_tables(pos_row, invf, groups):
    s = pos_row.shape[1]
    out = pl.BlockSpec((ROPE_TILE, LANES), lambda i: (i, 0))
    return pl.pallas_call(
        functools.partial(_rope_kernel, groups),
        out_shape=(jax.ShapeDtypeStruct((s, LANES), F32),) * 2,
        grid=(s // ROPE_TILE,),
        in_specs=[pl.BlockSpec((1, ROPE_TILE), lambda i: (0, i)), _const_spec(invf.shape)],
        out_specs=(out, out),
        compiler_params=_params(("parallel",)),
        name="rope_tables",
    )(pos_row, invf)


def _rope_pattern(rot_dim, offset, period):
    half = rot_dim // 2
    assert half % SUBLANES == 0 and offset % SUBLANES == 0 and period % SUBLANES == 0
    inv_freq = ROPE_THETA ** (-jnp.arange(0, rot_dim, 2, dtype=F32) / rot_dim)
    d = np.arange(LANES) % period - offset
    rotary = (d >= 0) & (d < rot_dim)
    sign = np.where(rotary, np.where(d < half, -1.0, 1.0), 0.0)
    groups = tuple((int(d[l] % half), int(sign[l])) if rotary[l] else None for l in range(0, LANES, SUBLANES))
    return inv_freq.reshape(half, 1), groups, jnp.asarray(sign, F32).reshape(1, LANES)


def _ffn_tile(x, g_ref, wgu_ref, wd_ref):
    h = _rms(x, g_ref[...]).astype(BF16)
    d_ff = wd_ref.shape[0]
    acc = x
    for c in range(d_ff // FFN_CHUNK):
        c0 = c * FFN_CHUNK
        gate = _dot(h, wgu_ref[:, c0:c0 + FFN_CHUNK])
        up = _dot(h, wgu_ref[:, d_ff + c0:d_ff + c0 + FFN_CHUNK])
        act = (gate * jax.nn.sigmoid(gate) * up).astype(BF16)
        acc = acc + _dot(act, wd_ref[c0:c0 + FFN_CHUNK, :])
    return acc


def _ffn_specs(ffn):
    layer = ffn[0]
    pick = lambda a: pl.BlockSpec((None,) + a.shape[1:], lambda *_: (layer,) + (0,) * (a.ndim - 1),
                                  pipeline_mode=pl.Buffered(1))
    return [pick(a) for a in ffn[1:]]


def _ffn_kernel(x_ref, g_ref, wgu_ref, wd_ref, o_ref):
    o_ref[...] = _ffn_tile(x_ref[...], g_ref, wgu_ref, wd_ref)


def _ffn(x, ffn):
    s = x.shape[0]
    row = pl.BlockSpec((ROW_TILE, D_MODEL), lambda i: (i, 0))
    return pl.pallas_call(
        _ffn_kernel,
        out_shape=jax.ShapeDtypeStruct((s, D_MODEL), F32),
        grid=(s // ROW_TILE,),
        in_specs=[row] + _ffn_specs(ffn),
        out_specs=row,
        compiler_params=_params(("parallel",)),
        name="swiglu_ffn",
    )(x, *ffn[1:])


def _block_ones(block, valid):
    r = np.arange(2 * LANES)
    same = (r[:, None] // block) == (r[None, :] // block)
    return jnp.asarray(same & ((r % block) < valid)[:, None], BF16)


def _a_proj_kernel(x_ref, g_ref, w_ref, gain_ref, ones_ref, cos_ref, sin_ref, sign_ref, q_ref, k_ref, v_ref, raw_ref):
    @pl.when(pl.program_id(0) == 0)
    def _():
        raw_ref[...] = jnp.zeros_like(raw_ref)

    lo = _lane_lo()
    nq = A_HEADS * A_HEAD_DIM
    nkv = A_KV_HEADS * A_HEAD_DIM
    cur = slice(0, ROW_TILE)
    qkv_cols = lambda c0, n: raw_ref[:, c0:c0 + n]
    first = sign_ref[...] < 0.0
    cos, sin = cos_ref[...], sin_ref[...]
    q_scale = A_HEAD_DIM ** -0.5 * LOG2E
    half_lanes = LANES // 2

    def split(col):
        a0 = jnp.where(lo, col, 0.0)
        b1 = jnp.where(lo, 0.0, col)
        return a0, pltpu.roll(a0, half_lanes, 1), pltpu.roll(b1, half_lanes, 1), b1

    for pr in range((nq + nkv) // (2 * LANES)):
        blk4 = qkv_cols(pr * 2 * LANES, 2 * LANES)
        ss = _dot((blk4 * blk4).astype(BF16), ones_ref[...])
        y = blk4 * lax.rsqrt(ss * (1.0 / A_HEAD_DIM) + EPS) * gain_ref[:, pr * 2 * LANES:(pr + 1) * 2 * LANES]
        for c in range(2):
            col = _rope_apply(y[:, c * LANES:(c + 1) * LANES], cos, sin, first, A_ROT_DIM // 2)
            cidx = 2 * pr + c
            if cidx < nq // LANES:
                q_ref[:, cidx * LANES:(cidx + 1) * LANES] = (col * q_scale).astype(BF16)
            else:
                kc = cidx - nq // LANES
                for t, part in enumerate(split(col)):
                    k_ref[cur, (4 * kc + t) * LANES:(4 * kc + t + 1) * LANES] = part.astype(BF16)
    vals = qkv_cols(nq + nkv, nkv)
    for c in range(nkv // LANES):
        a0, a1, b0, b1 = split(vals[:, c * LANES:(c + 1) * LANES])
        v_ref[cur, (2 * c) * LANES:(2 * c + 1) * LANES] = (a0 + a1).astype(BF16)
        v_ref[cur, (2 * c + 1) * LANES:(2 * c + 2) * LANES] = (b0 + b1).astype(BF16)
    raw_ref[...] = _dot(_rms(x_ref[...], g_ref[...]).astype(BF16), w_ref[...])


def _a_attn_kernel(sink_ref, q_ref, kp_ref, kc_ref, vp_ref, vc_ref, x_ref, wo_ref, gf_ref, wgu_ref, wd_ref, o_ref,
                   oacc_ref):
    i = pl.program_id(0)
    blk = A_WINDOW
    lo = _lane_lo()
    kcat = jnp.concatenate([kp_ref[...], kc_ref[...]], axis=0)
    vcat = jnp.concatenate([vp_ref[...], vc_ref[...]], axis=0)
    qi = lax.broadcasted_iota(jnp.int32, (2 * blk, 2 * blk), 0) & (blk - 1)
    kj = lax.broadcasted_iota(jnp.int32, (2 * blk, 2 * blk), 1)
    first_key = jnp.where(i > 0, 0, blk)
    ones = jnp.ones((2 * blk, LANES), BF16)
    for b in range(ROW_TILE // blk):
        valid = (kj > qi) & (kj <= qi + blk)
        if b == 0:
            valid = valid & (kj >= first_key)
        rows = slice(b * blk, (b + 1) * blk)
        win = slice(b * blk, (b + 2) * blk)
        for g in range(A_KV_HEADS):
            qs = jnp.concatenate([q_ref[rows, (2 * g) * LANES:(2 * g + 1) * LANES],
                                  q_ref[rows, (2 * g + 1) * LANES:(2 * g + 2) * LANES]], axis=0)
            vd = jnp.concatenate([vcat[win, g * LANES:(g + 1) * LANES], ones], axis=1)
            outs = []
            for half in range(2):
                kd = kcat[win, (2 * g + half) * LANES:(2 * g + half + 1) * LANES]
                sink = jnp.concatenate([jnp.full((blk, LANES), sink_ref[4 * g + half] * LOG2E, F32),
                                        jnp.full((blk, LANES), sink_ref[4 * g + 2 + half] * LOG2E, F32)], axis=0)
                s = jnp.where(valid, _dot_nt(qs, kd), -jnp.inf)
                s0, s1 = s[:, :LANES], s[:, LANES:]
                m = jnp.maximum(jnp.max(jnp.maximum(s0, s1), axis=-1, keepdims=True), sink)
                p = jnp.concatenate([jnp.exp2(s0 - m), jnp.exp2(s1 - m)], axis=1).astype(BF16)
                ol = _dot(p, vd)
                outs.append(ol[:, :LANES] / (ol[:, LANES:] + jnp.exp2(sink - m)))
            pair = jnp.where(lo, outs[0], outs[1])
            oacc_ref[rows, (2 * g) * LANES:(2 * g + 1) * LANES] = pair[:blk].astype(BF16)
            oacc_ref[rows, (2 * g + 1) * LANES:(2 * g + 2) * LANES] = pair[blk:].astype(BF16)

    o_ref[...] = _ffn_tile(x_ref[...] + _dot(oacc_ref[...], wo_ref[...]), gf_ref, wgu_ref, wd_ref)


def _mixer_a(x, g, w_qkv, q_norm, k_norm, sinks, w_o, tables, ffn):
    cos, sin, sign = tables
    s = x.shape[0]
    nq, nk, nv = A_HEADS * A_HEAD_DIM, 2 * A_KV_HEADS * LANES, A_KV_HEADS * LANES
    gain = jnp.concatenate([jnp.tile(q_norm, A_HEADS), jnp.tile(k_norm, A_KV_HEADS)]).reshape(1, -1)
    ones = _block_ones(A_HEAD_DIM, A_HEAD_DIM)
    w, wo = w_qkv.astype(BF16), w_o.astype(BF16)
    row = lambda n: pl.BlockSpec((ROW_TILE, n), lambda i: (i, 0))
    tiles = s // ROW_TILE
    ahead = lambda n: pl.BlockSpec((ROW_TILE, n), lambda i: (jnp.minimum(i, tiles - 1), 0))
    behind = lambda n: pl.BlockSpec((ROW_TILE, n), lambda i: (jnp.maximum(i - 1, 0), 0))
    q, k, v = pl.pallas_call(
        _a_proj_kernel,
        out_shape=(jax.ShapeDtypeStruct((s, nq), BF16), jax.ShapeDtypeStruct((s, nk), BF16),
                   jax.ShapeDtypeStruct((s, nv), BF16)),
        grid=(tiles + 1,),
        in_specs=[ahead(D_MODEL), _const_spec((1, D_MODEL)), _const_spec(w.shape), _const_spec(gain.shape),
                  _const_spec(ones.shape), behind(LANES), behind(LANES), _const_spec((1, LANES))],
        out_specs=(behind(nq), behind(nk), behind(nv)),
        scratch_shapes=[pltpu.VMEM((ROW_TILE, w.shape[1]), F32)],
        compiler_params=_params(("arbitrary",)),
        name="swa_qkv_proj",
    )(x, g.reshape(1, -1), w, gain, ones, cos, sin, sign)
    per = ROW_TILE // A_WINDOW
    prev = lambda n: pl.BlockSpec((A_WINDOW, n), lambda i: (jnp.maximum(i * per - 1, 0), 0))
    return pl.pallas_call(
        _a_attn_kernel,
        out_shape=jax.ShapeDtypeStruct((s, D_MODEL), F32),
        grid=(s // ROW_TILE,),
        in_specs=[pl.BlockSpec(memory_space=pltpu.SMEM), row(nq), prev(nk), row(nk), prev(nv), row(nv), row(D_MODEL),
                  _const_spec(wo.shape)] + _ffn_specs(ffn),
        out_specs=row(D_MODEL),
        scratch_shapes=[pltpu.VMEM((ROW_TILE, nq), BF16)],
        compiler_params=_params(("parallel",)),
        name="swa_attention_layer",
    )(sinks, q, k, k, v, v, x, wo, *ffn[1:])


def _b_kernel(x_ref, g_ref, win_ref, cw_ref, wout_ref, gf_ref, wgu_ref, wd_ref, o_ref, carry_ref):
    @pl.when(pl.program_id(0) == 0)
    def _():
        carry_ref[...] = jnp.zeros_like(carry_ref)

    x = x_ref[...]
    h = _rms(x, g_ref[...]).astype(BF16)
    bcu = _dot(h, win_ref[...])
    z = bcu[:, D_MODEL:2 * D_MODEL] * bcu[:, 2 * D_MODEL:]
    r = lax.broadcasted_iota(jnp.int32, (ROW_TILE, 1), 0)
    prev1 = carry_ref[SUBLANES - 1:SUBLANES, :]
    prev2 = carry_ref[SUBLANES - 2:SUBLANES - 1, :]
    z1 = jnp.where(r == 0, prev1, pltpu.roll(z, 1, 0))
    z2 = jnp.where(r == 0, prev2, jnp.where(r == 1, prev1, pltpu.roll(z, 2, 0)))
    y = cw_ref[0:1, :] * z2 + cw_ref[1:2, :] * z1 + cw_ref[2:3, :] * z
    carry_ref[...] = z[ROW_TILE - SUBLANES:]
    o_ref[...] = _ffn_tile(x + _dot((bcu[:, :D_MODEL] * y).astype(BF16), wout_ref[...]), gf_ref, wgu_ref, wd_ref)


def _mixer_b(x, g, w_in, conv_w, w_out, ffn):
    s = x.shape[0]
    row = pl.BlockSpec((ROW_TILE, D_MODEL), lambda i: (i, 0))
    return pl.pallas_call(
        _b_kernel,
        out_shape=jax.ShapeDtypeStruct((s, D_MODEL), F32),
        grid=(s // ROW_TILE,),
        in_specs=[row, _const_spec((1, D_MODEL)), _const_spec(w_in.shape), _const_spec(conv_w.shape),
                  _const_spec(w_out.shape)] + _ffn_specs(ffn),
        out_specs=row,
        scratch_shapes=[pltpu.VMEM((SUBLANES, D_MODEL), F32)],
        compiler_params=_params(("arbitrary",)),
        name="short_conv_layer",
    )(x, g.reshape(1, -1), w_in.astype(BF16), conv_w, w_out.astype(BF16), *ffn[1:])


def _c_proj_kernel(x_ref, g_ref, wd_ref, qan_ref, kvan_ref, wq_ref, wkv_ref, qn_ref, kn_ref, knp_ref, ones_ref,
                   cos_ref, sin_ref, q_ref, k_ref, v_ref):
    h = _rms(x_ref[...], g_ref[...]).astype(BF16)
    d = _dot(h, wd_ref[...])
    cq = _rms(d[:, :C_Q_RANK], qan_ref[...]).astype(BF16)
    lat = C_Q_RANK + C_KV_RANK
    ckv = _rms(d[:, C_Q_RANK:lat], kvan_ref[...]).astype(BF16)
    kr, krp = d[:, lat:lat + LANES], d[:, lat + LANES:]
    cos, sin = cos_ref[...], sin_ref[...]
    k_rot = (kr * kn_ref[...]) * cos + (krp * knp_ref[...]) * sin
    ss_rope = jnp.sum(kr * kr, axis=-1, keepdims=True)
    q_scale = C_DQK ** -0.5 * LOG2E
    qn2 = jnp.concatenate([qn_ref[...]] * 2, axis=1)
    for j in range(C_HEADS // 2):
        if j % 2 == 0:
            cols4 = slice(2 * j * LANES, (2 * j + 4) * LANES)
            qb4 = _dot(cq, wq_ref[:, cols4])
            kb4 = _dot(ckv, wkv_ref[:, cols4])
        half4 = slice((j % 2) * 2 * LANES, (j % 2 + 1) * 2 * LANES)
        qb, kb = qb4[:, half4], kb4[:, half4]
        ssq = _dot((qb * qb).astype(BF16), ones_ref[...])
        ssk = _dot((kb * kb).astype(BF16), ones_ref[...]) + ss_rope
        y = qb * lax.rsqrt(ssq * (1.0 / C_DQK) + EPS) * qn2
        rk = lax.rsqrt(ssk * (1.0 / C_DQK) + EPS)
        for c in range(2):
            lanes = slice(c * LANES, (c + 1) * LANES)
            yc = y[:, lanes]
            yc = yc * cos + pltpu.roll(yc, LANES - C_ROPE // 2, 1) * sin
            q_ref[2 * j + c] = (yc * q_scale).astype(BF16)
            k_ref[2 * j + c] = (rk[:, lanes] * (kb[:, lanes] * kn_ref[...] + k_rot)).astype(BF16)
    v_all = _dot(ckv, wkv_ref[:, C_HEADS * LANES:])
    for j in range(C_HEADS // 2):
        v_ref[j] = v_all[:, j * LANES:(j + 1) * LANES].astype(BF16)


def _c_proj(x, g, wd, qan, kvan, wq, wkv, qn, kn, knp, ones, cos, sin):
    s = x.shape[0]
    row = lambda n: pl.BlockSpec((ROW_TILE, n), lambda i: (i, 0))
    heads = lambda n: pl.BlockSpec((n, ROW_TILE, LANES), lambda i: (0, i, 0))
    return pl.pallas_call(
        _c_proj_kernel,
        out_shape=(jax.ShapeDtypeStruct((C_HEADS, s, LANES), BF16), jax.ShapeDtypeStruct((C_HEADS, s, LANES), BF16),
                   jax.ShapeDtypeStruct((C_HEADS // 2, s, LANES), BF16)),
        grid=(s // ROW_TILE,),
        in_specs=[row(D_MODEL), _const_spec((1, D_MODEL)), _const_spec(wd.shape), _const_spec(qan.shape),
                  _const_spec(kvan.shape), _const_spec(wq.shape), _const_spec(wkv.shape), _const_spec((1, LANES)),
                  _const_spec((1, LANES)), _const_spec((1, LANES)), _const_spec(ones.shape), row(LANES), row(LANES)],
        out_specs=(heads(C_HEADS), heads(C_HEADS), heads(C_HEADS // 2)),
        compiler_params=_params(("parallel",)),
        name="mla_proj",
    )(x, g, wd, qan, kvan, wq, wkv, qn, kn, knp, ones, cos, sin)


def _c_flash_kernel(qi_ref, ki_ref, kind_ref, last_ref, q_ref, k_ref, v_ref, x_ref, wo_ref, o_ref,
                    m_ref, l_ref, acc_ref, s0_ref, s1_ref, p0_ref, p1_ref, a0_ref, a1_ref):
    step = pl.program_id(0)
    qi, ki = qi_ref[step], ki_ref[step]
    lo = _lane_lo()
    s_bufs, p_bufs, a_bufs = (s0_ref, s1_ref), (p0_ref, p1_ref), (a0_ref, a1_ref)

    @pl.when(ki == 0)
    def _():
        m_ref[...] = jnp.full_like(m_ref, -jnp.inf)
        l_ref[...] = jnp.zeros_like(l_ref)
        acc_ref[...] = jnp.zeros_like(acc_ref)

    def sweep(sub_blocks):
        items = [(kb, h) for kb in range(len(sub_blocks)) for h in range(C_HEADS)]
        keys = lambda kb: slice(kb * MLA_TK, (kb + 1) * MLA_TK)
        causal = {}
        for kb, masked in enumerate(sub_blocks):
            if masked:
                qpos = qi * MLA_TQ + lax.broadcasted_iota(jnp.int32, (MLA_TQ, MLA_TK), 0)
                kpos = (ki * MLA_KSPLIT + kb) * MLA_TK + lax.broadcasted_iota(jnp.int32, (MLA_TQ, MLA_TK), 1)
                causal[kb] = kpos <= qpos

        def scores(item, slot):
            kb, h = item
            s = _dot_nt(q_ref[h], k_ref[h, keys(kb), :])
            if kb in causal:
                s = jnp.where(causal[kb], s, -jnp.inf)
            s_bufs[slot][...] = s

        def softmax(item, slot):
            h = item[1]
            s_ref, p_ref = s_bufs[slot], p_bufs[slot]
            for rb in range(MLA_TQ // SOFTMAX_ROWS):
                rows = slice(rb * SOFTMAX_ROWS, (rb + 1) * SOFTMAX_ROWS)
                chunks = [slice(c * LANES, (c + 1) * LANES) for c in range(MLA_TK // LANES)]
                part = s_ref[rows, chunks[0]]
                for cols in chunks[1:]:
                    part = jnp.maximum(part, s_ref[rows, cols])
                m_prev = m_ref[h, rows, :]
                m_new = jnp.maximum(m_prev, jnp.max(part, axis=-1, keepdims=True))
                alpha = jnp.exp2(m_prev - m_new)
                lsum = alpha * l_ref[h, rows, :]
                for cols in chunks:
                    p = jnp.exp2(s_ref[rows, cols] - m_new)
                    lsum = lsum + p
                    p_ref[rows, cols] = p.astype(BF16)
                l_ref[h, rows, :] = lsum
                m_ref[h, rows, :] = m_new
                a_bufs[slot][rows, :] = alpha

        def pv(item, slot):
            kb, h = item
            j = h // 2
            keep = lo if h % 2 == 0 else jnp.logical_not(lo)
            acc = acc_ref[j]
            pv_h = _dot(p_bufs[slot][...], v_ref[j, keys(kb), :])
            acc_ref[j] = jnp.where(keep, acc * a_bufs[slot][...] + pv_h, acc)

        for t in range(len(items) + 2):
            if t < len(items):
                scores(items[t], t % 2)
            if 1 <= t <= len(items):
                softmax(items[t - 1], (t - 1) % 2)
            if t >= 2:
                pv(items[t - 2], t % 2)

    for kind, sub_blocks in enumerate(FLASH_STEP_KINDS):
        @pl.when(kind_ref[step] == kind)
        def _():
            sweep(sub_blocks)

    @pl.when(last_ref[step] == 1)
    def _():
        cols = []
        for j in range(C_HEADS // 2):
            l_even = jnp.sum(l_ref[2 * j], axis=-1, keepdims=True)
            l_odd = jnp.sum(l_ref[2 * j + 1], axis=-1, keepdims=True)
            cols.append((acc_ref[j] / jnp.where(lo, l_even, l_odd)).astype(BF16))
        o_ref[...] = x_ref[...] + _dot(jnp.concatenate(cols, axis=1), wo_ref[...])


def _flash_schedule(s):
    assert MLA_TQ == MLA_TK
    qi, ki, kind, last = [], [], [], []
    for a in range(s // MLA_TQ):
        nk = a // MLA_KSPLIT + 1
        for b in range(nk):
            subs = [b * MLA_KSPLIT + kb for kb in range(MLA_KSPLIT) if b * MLA_KSPLIT + kb <= a]
            qi.append(a)
            ki.append(b)
            kind.append(FLASH_STEP_KINDS.index(tuple(g == a for g in subs)))
            last.append(int(b == nk - 1))
    return tuple(jnp.asarray(v, jnp.int32) for v in (qi, ki, kind, last))


def _c_flash(q, k, v, x, wo):
    s = x.shape[0]
    qi, ki, kind, last = _flash_schedule(s)
    key_rows = MLA_TK * MLA_KSPLIT
    grid_spec = pltpu.PrefetchScalarGridSpec(
        num_scalar_prefetch=4,
        grid=(int(qi.shape[0]),),
        in_specs=[
            pl.BlockSpec((C_HEADS, MLA_TQ, LANES), lambda t, qi, ki, kind, last: (0, qi[t], 0)),
            pl.BlockSpec((C_HEADS, key_rows, LANES), lambda t, qi, ki, kind, last: (0, ki[t], 0)),
            pl.BlockSpec((C_HEADS // 2, key_rows, LANES), lambda t, qi, ki, kind, last: (0, ki[t], 0)),
            pl.BlockSpec((MLA_TQ, D_MODEL), lambda t, qi, ki, kind, last: (qi[t], 0)),
            pl.BlockSpec(wo.shape, lambda t, qi, ki, kind, last: (0, 0), pipeline_mode=pl.Buffered(1)),
        ],
        out_specs=pl.BlockSpec((MLA_TQ, D_MODEL), lambda t, qi, ki, kind, last: (qi[t], 0)),
        scratch_shapes=[pltpu.VMEM((C_HEADS, MLA_TQ, LANES), F32), pltpu.VMEM((C_HEADS, MLA_TQ, LANES), F32),
                        pltpu.VMEM((C_HEADS // 2, MLA_TQ, LANES), F32),
                        pltpu.VMEM((MLA_TQ, MLA_TK), F32), pltpu.VMEM((MLA_TQ, MLA_TK), F32),
                        pltpu.VMEM((MLA_TQ, MLA_TK), BF16), pltpu.VMEM((MLA_TQ, MLA_TK), BF16),
                        pltpu.VMEM((MLA_TQ, LANES), F32), pltpu.VMEM((MLA_TQ, LANES), F32)],
    )
    return pl.pallas_call(
        _c_flash_kernel,
        out_shape=jax.ShapeDtypeStruct((s, D_MODEL), F32),
        grid_spec=grid_spec,
        compiler_params=_params(("arbitrary",)),
        name="mla_flash",
    )(qi, ki, kind, last, q, k, v, x, wo)


def _pad_lanes(v, n=LANES):
    return jnp.pad(v, (0, n - v.shape[0])).reshape(1, n)


def _mixer_c(x, g, w_down, q_a_norm, kv_a_norm, w_q_up, w_kv_up, q_norm, k_norm, w_o, tables, ffn):
    cos, sin, _ = tables
    half = C_ROPE // 2
    lat = C_Q_RANK + C_KV_RANK
    w_rope = w_down[:, lat:]
    w_rope_partner = jnp.concatenate([w_rope[:, half:], w_rope[:, :half]], axis=1)
    at_rope_lanes = lambda w: jnp.pad(w, ((0, 0), (C_NOPE, LANES - C_DQK)))
    wd = jnp.concatenate([w_down[:, :lat], at_rope_lanes(w_rope), at_rope_lanes(w_rope_partner)], axis=1).astype(BF16)
    wq = w_q_up.reshape(C_Q_RANK, C_HEADS, C_DQK)
    wq = jnp.concatenate([wq, wq[:, :, C_NOPE:C_NOPE + half], jnp.zeros((C_Q_RANK, C_HEADS, half), F32)], axis=2)
    wq = wq.reshape(C_Q_RANK, C_HEADS * LANES).astype(BF16)
    wkv = w_kv_up.reshape(C_KV_RANK, C_HEADS, C_NOPE + C_V)
    wk = jnp.pad(wkv[:, :, :C_NOPE], ((0, 0), (0, 0), (0, LANES - C_NOPE))).reshape(C_KV_RANK, C_HEADS * LANES)
    wv = wkv[:, :, C_NOPE:].reshape(C_KV_RANK, C_HEADS * C_V)
    wkv = jnp.concatenate([wk, wv], axis=1).astype(BF16)
    qn = _pad_lanes(jnp.concatenate([q_norm, q_norm[C_NOPE:C_NOPE + half]]))
    kn_rope = k_norm[C_NOPE:]
    knp = _pad_lanes(jnp.concatenate([jnp.zeros((C_NOPE,), F32), kn_rope[half:], kn_rope[:half]]))
    q, k, v = _c_proj(x, g.reshape(1, -1), wd, q_a_norm.reshape(1, -1), kv_a_norm.reshape(1, -1), wq, wkv,
                      qn, _pad_lanes(k_norm), knp, _block_ones(LANES, C_DQK), cos, sin)
    return _ffn(_c_flash(q, k, v, x, w_o.astype(BF16)), ffn)


def kernel(x, positions, mix_norm, ffn_norm, a_w_qkv, a_q_norm, a_k_norm, a_sinks, a_w_o, b_w_in, b_conv_w, b_w_out,
           c_w_down, c_q_a_norm, c_kv_a_norm, c_w_q_up, c_w_kv_up, c_q_norm, c_k_norm, c_w_o, f_w_gate_up, f_w_down):
    batch, seq, d = x.shape
    assert batch == 1 and d == D_MODEL and seq % max(ROW_TILE, ROPE_TILE, MLA_TQ, MLA_TK * MLA_KSPLIT) == 0
    depth = mix_norm.shape[0]
    pos_row = positions.reshape(1, seq)
    invf_a, groups_a, sign_a = _rope_pattern(A_ROT_DIM, 0, A_HEAD_DIM)
    invf_c, groups_c, sign_c = _rope_pattern(C_ROPE, C_NOPE, LANES)
    tab_a = _rope_tables(pos_row, invf_a, groups_a) + (sign_a,)
    tab_c = _rope_tables(pos_row, invf_c, groups_c) + (sign_c,)
    xs = x.reshape(seq, d)
    ffn_all = (ffn_norm.reshape(depth, 1, d), f_w_gate_up.astype(BF16), f_w_down.astype(BF16))
    for i in range(depth):
        kind, j = i % 3, i // 3
        ffn = (i,) + ffn_all
        if kind == 0:
            xs = _mixer_a(xs, mix_norm[i], a_w_qkv[j], a_q_norm[j], a_k_norm[j], a_sinks[j], a_w_o[j], tab_a, ffn)
        elif kind == 1:
            xs = _mixer_b(xs, mix_norm[i], b_w_in[j], b_conv_w[j], b_w_out[j], ffn)
        else:
            xs = _mixer_c(xs, mix_norm[i], c_w_down[j], c_q_a_norm[j], c_kv_a_norm[j], c_w_q_up[j], c_w_kv_up[j],
                          c_q_norm[j], c_k_norm[j], c_w_o[j], tab_c, ffn)
    return xs.reshape(batch, seq, d)
```

```python
import functools
import math

import jax
import jax.numpy as jnp
import numpy as np
from jax import lax
from jax.experimental import pallas as pl
from jax.experimental.pallas import tpu as pltpu

F32 = jnp.float32
BF16 = jnp.bfloat16

D_MODEL = 1024
ROPE_THETA = 500000.0
EPS = 1e-6
A_HEADS, A_KV_HEADS, A_HEAD_DIM, A_ROT_DIM, A_WINDOW = 16, 4, 64, 16, 128
C_HEADS, C_NOPE, C_ROPE, C_V, C_Q_RANK, C_KV_RANK = 16, 64, 32, 64, 384, 256
C_DQK = C_NOPE + C_ROPE
LOG2E = math.log2(math.e)

LANES = 128
SUBLANES = 8
VMEM_LIMIT_BYTES = 56 * 2**20

ROW_TILE = 512
ROPE_TILE = 2048
FFN_CHUNK = 256
MLA_TQ = 512
MLA_TK = 512
MLA_KSPLIT = 2
FLASH_STEP_KINDS = ((False, False), (False, True), (True,))
SOFTMAX_ROWS = 64


def _params(sem):
    return pltpu.CompilerParams(dimension_semantics=sem, vmem_limit_bytes=VMEM_LIMIT_BYTES)


def _const_spec(shape):
    return pl.BlockSpec(shape, lambda *_: (0,) * len(shape), pipeline_mode=pl.Buffered(1))


def _rms(x, g):
    ms = jnp.mean(x * x, axis=-1, keepdims=True)
    return x * lax.rsqrt(ms + EPS) * g


def _dot(a, b):
    return jnp.dot(a, b, preferred_element_type=F32)


def _dot_nt(a, b):
    return lax.dot_general(a, b, (((1,), (1,)), ((), ())), preferred_element_type=F32)


def _lane_lo():
    return lax.broadcasted_iota(jnp.int32, (1, LANES), 1) < (LANES // 2)


def _rope_apply(x, cos, sin, first_half, half):
    fwd = pltpu.roll(x, LANES - half, 1)
    bwd = pltpu.roll(x, half, 1)
    return x * cos + jnp.where(first_half, fwd, bwd) * sin


def _rope_kernel(groups, pos_ref, invf_ref, cos_ref, sin_ref):
    ang = invf_ref[...] * pos_ref[...].astype(F32)
    c, s = jnp.cos(ang), jnp.sin(ang)
    ones = jnp.ones((SUBLANES, ROPE_TILE), F32)
    zeros = jnp.zeros((SUBLANES, ROPE_TILE), F32)
    cos_rows, sin_rows = [], []
    for grp in groups:
        if grp is None:
            cos_rows.append(ones)
            sin_rows.append(zeros)
        else:
            r0, sign = grp
            cos_rows.append(c[r0:r0 + SUBLANES])
            sin_rows.append(s[r0:r0 + SUBLANES] if sign > 0 else -s[r0:r0 + SUBLANES])
    cos_ref[...] = jnp.concatenate(cos_rows, axis=0).T
    sin_ref[...] = jnp.concatenate(sin_rows, axis=0).T


def _rope_tables(pos_row, invf, groups):
    s = pos_row.shape[1]
    out = pl.BlockSpec((ROPE_TILE, LANES), lambda i: (i, 0))
    return pl.pallas_call(
        functools.partial(_rope_kernel, groups),
        out_shape=(jax.ShapeDtypeStruct((s, LANES), F32),) * 2,
        grid=(s // ROPE_TILE,),
        in_specs=[pl.BlockSpec((1, ROPE_TILE), lambda i: (0, i)), _const_spec(invf.shape)],
        out_specs=(out, out),
        compiler_params=_params(("parallel",)),
        name="rope_tables",
    )(pos_row, invf)


def _rope_pattern(rot_dim, offset, period):
    half = rot_dim // 2
    assert half % SUBLANES == 0 and offset % SUBLANES == 0 and period % SUBLANES == 0
    inv_freq = ROPE_THETA ** (-jnp.arange(0, rot_dim, 2, dtype=F32) / rot_dim)
    d = np.arange(LANES) % period - offset
    rotary = (d >= 0) & (d < rot_dim)
    sign = np.where(rotary, np.where(d < half, -1.0, 1.0), 0.0)
    groups = tuple((int(d[l] % half), int(sign[l])) if rotary[l] else None for l in range(0, LANES, SUBLANES))
    return inv_freq.reshape(half, 1), groups, jnp.asarray(sign, F32).reshape(1, LANES)


def _ffn_tile(x, g_ref, wgu_ref, wd_ref):
    h = _rms(x, g_ref[...]).astype(BF16)
    d_ff = wd_ref.shape[0]
    acc = x
    for c in range(d_ff // FFN_CHUNK):
        c0 = c * FFN_CHUNK
        gate = _dot(h, wgu_ref[:, c0:c0 + FFN_CHUNK])
        up = _dot(h, wgu_ref[:, d_ff + c0:d_ff + c0 + FFN_CHUNK])
        act = (gate * jax.nn.sigmoid(gate) * up).astype(BF16)
        acc = acc + _dot(act, wd_ref[c0:c0 + FFN_CHUNK, :])
    return acc


def _ffn_specs(ffn):
    layer = ffn[0]
    pick = lambda a: pl.BlockSpec((None,) + a.shape[1:], lambda *_: (layer,) + (0,) * (a.ndim - 1),
                                  pipeline_mode=pl.Buffered(1))
    return [pick(a) for a in ffn[1:]]


def _ffn_kernel(x_ref, g_ref, wgu_ref, wd_ref, o_ref):
    o_ref[...] = _ffn_tile(x_ref[...], g_ref, wgu_ref, wd_ref)


def _ffn(x, ffn):
    s = x.shape[0]
    row = pl.BlockSpec((ROW_TILE, D_MODEL), lambda i: (i, 0))
    return pl.pallas_call(
        _ffn_kernel,
        out_shape=jax.ShapeDtypeStruct((s, D_MODEL), F32),
        grid=(s // ROW_TILE,),
        in_specs=[row] + _ffn_specs(ffn),
        out_specs=row,
        compiler_params=_params(("parallel",)),
        name="swiglu_ffn",
    )(x, *ffn[1:])


def _block_ones(block, valid):
    r = np.arange(2 * LANES)
    same = (r[:, None] // block) == (r[None, :] // block)
    return jnp.asarray(same & ((r % block) < valid)[:, None], BF16)


def _a_proj_kernel(x_ref, g_ref, w_ref, gain_ref, ones_ref, cos_ref, sin_ref, sign_ref, q_ref, k_ref, v_ref, raw_ref):
    @pl.when(pl.program_id(0) == 0)
    def _():
        raw_ref[...] = jnp.zeros_like(raw_ref)

    lo = _lane_lo()
    nq = A_HEADS * A_HEAD_DIM
    nkv = A_KV_HEADS * A_HEAD_DIM
    cur = slice(0, ROW_TILE)
    qkv_cols = lambda c0, n: raw_ref[:, c0:c0 + n]
    first = sign_ref[...] < 0.0
    cos, sin = cos_ref[...], sin_ref[...]
    q_scale = A_HEAD_DIM ** -0.5 * LOG2E
    half_lanes = LANES // 2

    def split(col):
        a0 = jnp.where(lo, col, 0.0)
        b1 = jnp.where(lo, 0.0, col)
        return a0, pltpu.roll(a0, half_lanes, 1), pltpu.roll(b1, half_lanes, 1), b1

    for pr in range((nq + nkv) // (2 * LANES)):
        blk4 = qkv_cols(pr * 2 * LANES, 2 * LANES)
        ss = _dot((blk4 * blk4).astype(BF16), ones_ref[...])
        y = blk4 * lax.rsqrt(ss * (1.0 / A_HEAD_DIM) + EPS) * gain_ref[:, pr * 2 * LANES:(pr + 1) * 2 * LANES]
        for c in range(2):
            col = _rope_apply(y[:, c * LANES:(c + 1) * LANES], cos, sin, first, A_ROT_DIM // 2)
            cidx = 2 * pr + c
            if cidx < nq // LANES:
                q_ref[:, cidx * LANES:(cidx + 1) * LANES] = (col * q_scale).astype(BF16)
            else:
                kc = cidx - nq // LANES
                for t, part in enumerate(split(col)):
                    k_ref[cur, (4 * kc + t) * LANES:(4 * kc + t + 1) * LANES] = part.astype(BF16)
    vals = qkv_cols(nq + nkv, nkv)
    for c in range(nkv // LANES):
        a0, a1, b0, b1 = split(vals[:, c * LANES:(c + 1) * LANES])
        v_ref[cur, (2 * c) * LANES:(2 * c + 1) * LANES] = (a0 + a1).astype(BF16)
        v_ref[cur, (2 * c + 1) * LANES:(2 * c + 2) * LANES] = (b0 + b1).astype(BF16)
    raw_ref[...] = _dot(_rms(x_ref[...], g_ref[...]).astype(BF16), w_ref[...])


def _a_attn_kernel(sink_ref, q_ref, kp_ref, kc_ref, vp_ref, vc_ref, x_ref, wo_ref, gf_ref, wgu_ref, wd_ref, o_ref,
                   oacc_ref):
    i = pl.program_id(0)
    blk = A_WINDOW
    lo = _lane_lo()
    kcat = jnp.concatenate([kp_ref[...], kc_ref[...]], axis=0)
    vcat = jnp.concatenate([vp_ref[...], vc_ref[...]], axis=0)
    qi = lax.broadcasted_iota(jnp.int32, (2 * blk, 2 * blk), 0) & (blk - 1)
    kj = lax.broadcasted_iota(jnp.int32, (2 * blk, 2 * blk), 1)
    first_key = jnp.where(i > 0, 0, blk)
    ones = jnp.ones((2 * blk, LANES), BF16)
    for b in range(ROW_TILE // blk):
        valid = (kj > qi) & (kj <= qi + blk)
        if b == 0:
            valid = valid & (kj >= first_key)
        rows = slice(b * blk, (b + 1) * blk)
        win = slice(b * blk, (b + 2) * blk)
        for g in range(A_KV_HEADS):
            qs = jnp.concatenate([q_ref[rows, (2 * g) * LANES:(2 * g + 1) * LANES],
                                  q_ref[rows, (2 * g + 1) * LANES:(2 * g + 2) * LANES]], axis=0)
            vd = jnp.concatenate([vcat[win, g * LANES:(g + 1) * LANES], ones], axis=1)
            outs = []
            for half in range(2):
                kd = kcat[win, (2 * g + half) * LANES:(2 * g + half + 1) * LANES]
                sink = jnp.concatenate([jnp.full((blk, LANES), sink_ref[4 * g + half] * LOG2E, F32),
                                        jnp.full((blk, LANES), sink_ref[4 * g + 2 + half] * LOG2E, F32)], axis=0)
                s = jnp.where(valid, _dot_nt(qs, kd), -jnp.inf)
                s0, s1 = s[:, :LANES], s[:, LANES:]
                m = jnp.maximum(jnp.max(jnp.maximum(s0, s1), axis=-1, keepdims=True), sink)
                p = jnp.concatenate([jnp.exp2(s0 - m), jnp.exp2(s1 - m)], axis=1).astype(BF16)
                ol = _dot(p, vd)
                outs.append(ol[:, :LANES] / (ol[:, LANES:] + jnp.exp2(sink - m)))
            pair = jnp.where(lo, outs[0], outs[1])
            oacc_ref[rows, (2 * g) * LANES:(2 * g + 1) * LANES] = pair[:blk].astype(BF16)
            oacc_ref[rows, (2 * g + 1) * LANES:(2 * g + 2) * LANES] = pair[blk:].astype(BF16)

    o_ref[...] = _ffn_tile(x_ref[...] + _dot(oacc_ref[...], wo_ref[...]), gf_ref, wgu_ref, wd_ref)


def _mixer_a(x, g, w_qkv, q_norm, k_norm, sinks, w_o, tables, ffn):
    cos, sin, sign = tables
    s = x.shape[0]
    nq, nk, nv = A_HEADS * A_HEAD_DIM, 2 * A_KV_HEADS * LANES, A_KV_HEADS * LANES
    gain = jnp.concatenate([jnp.tile(q_norm, A_HEADS), jnp.tile(k_norm, A_KV_HEADS)]).reshape(1, -1)
    ones = _block_ones(A_HEAD_DIM, A_HEAD_DIM)
    w, wo = w_qkv.astype(BF16), w_o.astype(BF16)
    row = lambda n: pl.BlockSpec((ROW_TILE, n), lambda i: (i, 0))
    tiles = s // ROW_TILE
    ahead = lambda n: pl.BlockSpec((ROW_TILE, n), lambda i: (jnp.minimum(i, tiles - 1), 0))
    behind = lambda n: pl.BlockSpec((ROW_TILE, n), lambda i: (jnp.maximum(i - 1, 0), 0))
    q, k, v = pl.pallas_call(
        _a_proj_kernel,
        out_shape=(jax.ShapeDtypeStruct((s, nq), BF16), jax.ShapeDtypeStruct((s, nk), BF16),
                   jax.ShapeDtypeStruct((s, nv), BF16)),
        grid=(tiles + 1,),
        in_specs=[ahead(D_MODEL), _const_spec((1, D_MODEL)), _const_spec(w.shape), _const_spec(gain.shape),
                  _const_spec(ones.shape), behind(LANES), behind(LANES), _const_spec((1, LANES))],
        out_specs=(behind(nq), behind(nk), behind(nv)),
        scratch_shapes=[pltpu.VMEM((ROW_TILE, w.shape[1]), F32)],
        compiler_params=_params(("arbitrary",)),
        name="swa_qkv_proj",
    )(x, g.reshape(1, -1), w, gain, ones, cos, sin, sign)
    per = ROW_TILE // A_WINDOW
    prev = lambda n: pl.BlockSpec((A_WINDOW, n), lambda i: (jnp.maximum(i * per - 1, 0), 0))
    return pl.pallas_call(
        _a_attn_kernel,
        out_shape=jax.ShapeDtypeStruct((s, D_MODEL), F32),
        grid=(s // ROW_TILE,),
        in_specs=[pl.BlockSpec(memory_space=pltpu.SMEM), row(nq), prev(nk), row(nk), prev(nv), row(nv), row(D_MODEL),
                  _const_spec(wo.shape)] + _ffn_specs(ffn),
        out_specs=row(D_MODEL),
        scratch_shapes=[pltpu.VMEM((ROW_TILE, nq), BF16)],
        compiler_params=_params(("parallel",)),
        name="swa_attention_layer",
    )(sinks, q, k, k, v, v, x, wo, *ffn[1:])


def _b_kernel(x_ref, g_ref, win_ref, cw_ref, wout_ref, gf_ref, wgu_ref, wd_ref, o_ref, carry_ref):
    @pl.when(pl.program_id(0) == 0)
    def _():
        carry_ref[...] = jnp.zeros_like(carry_ref)

    x = x_ref[...]
    h = _rms(x, g_ref[...]).astype(BF16)
    bcu = _dot(h, win_ref[...])
    z = bcu[:, D_MODEL:2 * D_MODEL] * bcu[:, 2 * D_MODEL:]
    r = lax.broadcasted_iota(jnp.int32, (ROW_TILE, 1), 0)
    prev1 = carry_ref[SUBLANES - 1:SUBLANES, :]
    prev2 = carry_ref[SUBLANES - 2:SUBLANES - 1, :]
    z1 = jnp.where(r == 0, prev1, pltpu.roll(z, 1, 0))
    z2 = jnp.where(r == 0, prev2, jnp.where(r == 1, prev1, pltpu.roll(z, 2, 0)))
    y = cw_ref[0:1, :] * z2 + cw_ref[1:2, :] * z1 + cw_ref[2:3, :] * z
    carry_ref[...] = z[ROW_TILE - SUBLANES:]
    o_ref[...] = _ffn_tile(x + _dot((bcu[:, :D_MODEL] * y).astype(BF16), wout_ref[...]), gf_ref, wgu_ref, wd_ref)


def _mixer_b(x, g, w_in, conv_w, w_out, ffn):
    s = x.shape[0]
    row = pl.BlockSpec((ROW_TILE, D_MODEL), lambda i: (i, 0))
    return pl.pallas_call(
        _b_kernel,
        out_shape=jax.ShapeDtypeStruct((s, D_MODEL), F32),
        grid=(s // ROW_TILE,),
        in_specs=[row, _const_spec((1, D_MODEL)), _const_spec(w_in.shape), _const_spec(conv_w.shape),
                  _const_spec(w_out.shape)] + _ffn_specs(ffn),
        out_specs=row,
        scratch_shapes=[pltpu.VMEM((SUBLANES, D_MODEL), F32)],
        compiler_params=_params(("arbitrary",)),
        name="short_conv_layer",
    )(x, g.reshape(1, -1), w_in.astype(BF16), conv_w, w_out.astype(BF16), *ffn[1:])


def _c_proj_kernel(x_ref, g_ref, wd_ref, qan_ref, kvan_ref, wq_ref, wkv_ref, qn_ref, kn_ref, knp_ref, ones_ref,
                   cos_ref, sin_ref, q_ref, k_ref, v_ref, cq_ref, ckv_ref, rope_ref):
    @pl.when(pl.program_id(0) == 0)
    def _():
        cq_ref[...] = jnp.zeros_like(cq_ref)
        ckv_ref[...] = jnp.zeros_like(ckv_ref)
        rope_ref[...] = jnp.zeros_like(rope_ref)

    lat = C_Q_RANK + C_KV_RANK
    cq, ckv = cq_ref[...], ckv_ref[...]
    kr, krp = rope_ref[:, :LANES], rope_ref[:, LANES:]
    cos, sin = cos_ref[...], sin_ref[...]
    k_rot = (kr * kn_ref[...]) * cos + (krp * knp_ref[...]) * sin
    ss_rope = jnp.sum(kr * kr, axis=-1, keepdims=True)
    q_scale = C_DQK ** -0.5 * LOG2E
    qn2 = jnp.concatenate([qn_ref[...]] * 2, axis=1)
    for j in range(C_HEADS // 2):
        if j % 2 == 0:
            cols4 = slice(2 * j * LANES, (2 * j + 4) * LANES)
            qb4 = _dot(cq, wq_ref[:, cols4])
            kb4 = _dot(ckv, wkv_ref[:, cols4])
        half4 = slice((j % 2) * 2 * LANES, (j % 2 + 1) * 2 * LANES)
        qb, kb = qb4[:, half4], kb4[:, half4]
        ssq = _dot((qb * qb).astype(BF16), ones_ref[...])
        ssk = _dot((kb * kb).astype(BF16), ones_ref[...]) + ss_rope
        y = qb * lax.rsqrt(ssq * (1.0 / C_DQK) + EPS) * qn2
        rk = lax.rsqrt(ssk * (1.0 / C_DQK) + EPS)
        for c in range(2):
            lanes = slice(c * LANES, (c + 1) * LANES)
            yc = y[:, lanes]
            yc = yc * cos + pltpu.roll(yc, LANES - C_ROPE // 2, 1) * sin
            q_ref[2 * j + c] = (yc * q_scale).astype(BF16)
            k_ref[2 * j + c] = (rk[:, lanes] * (kb[:, lanes] * kn_ref[...] + k_rot)).astype(BF16)
    v_all = _dot(ckv, wkv_ref[:, C_HEADS * LANES:])
    for j in range(C_HEADS // 2):
        v_ref[j] = v_all[:, j * LANES:(j + 1) * LANES].astype(BF16)

    h = _rms(x_ref[...], g_ref[...]).astype(BF16)
    d = _dot(h, wd_ref[...])
    cq_ref[...] = _rms(d[:, :C_Q_RANK], qan_ref[...]).astype(BF16)
    ckv_ref[...] = _rms(d[:, C_Q_RANK:lat], kvan_ref[...]).astype(BF16)
    rope_ref[...] = d[:, lat:]


def _c_proj(x, g, wd, qan, kvan, wq, wkv, qn, kn, knp, ones, cos, sin):
    s = x.shape[0]
    tiles = s // ROW_TILE
    ahead = lambda i: jnp.minimum(i, tiles - 1)
    behind = lambda i: jnp.maximum(i - 1, 0)
    heads = lambda n: pl.BlockSpec((n, ROW_TILE, LANES), lambda i: (0, behind(i), 0))
    table = pl.BlockSpec((ROW_TILE, LANES), lambda i: (behind(i), 0))
    return pl.pallas_call(
        _c_proj_kernel,
        out_shape=(jax.ShapeDtypeStruct((C_HEADS, s, LANES), BF16), jax.ShapeDtypeStruct((C_HEADS, s, LANES), BF16),
                   jax.ShapeDtypeStruct((C_HEADS // 2, s, LANES), BF16)),
        grid=(tiles + 1,),
        in_specs=[pl.BlockSpec((ROW_TILE, D_MODEL), lambda i: (ahead(i), 0)), _const_spec((1, D_MODEL)),
                  _const_spec(wd.shape), _const_spec(qan.shape), _const_spec(kvan.shape), _const_spec(wq.shape),
                  _const_spec(wkv.shape), _const_spec((1, LANES)), _const_spec((1, LANES)), _const_spec((1, LANES)),
                  _const_spec(ones.shape), table, table],
        out_specs=(heads(C_HEADS), heads(C_HEADS), heads(C_HEADS // 2)),
        scratch_shapes=[pltpu.VMEM((ROW_TILE, C_Q_RANK), BF16), pltpu.VMEM((ROW_TILE, C_KV_RANK), BF16),
                        pltpu.VMEM((ROW_TILE, 2 * LANES), F32)],
        compiler_params=_params(("arbitrary",)),
        name="mla_proj",
    )(x, g, wd, qan, kvan, wq, wkv, qn, kn, knp, ones, cos, sin)


def _c_flash_kernel(qi_ref, ki_ref, kind_ref, last_ref, q_ref, k_ref, v_ref, x_ref, wo_ref, o_ref,
                    m_ref, l_ref, acc_ref, s0_ref, s1_ref, p0_ref, p1_ref, a0_ref, a1_ref):
    step = pl.program_id(0)
    qi, ki = qi_ref[step], ki_ref[step]
    lo = _lane_lo()
    s_bufs, p_bufs, a_bufs = (s0_ref, s1_ref), (p0_ref, p1_ref), (a0_ref, a1_ref)

    @pl.when(ki == 0)
    def _():
        m_ref[...] = jnp.full_like(m_ref, -jnp.inf)
        l_ref[...] = jnp.zeros_like(l_ref)
        acc_ref[...] = jnp.zeros_like(acc_ref)

    def sweep(sub_blocks):
        items = [(kb, h) for kb in range(len(sub_blocks)) for h in range(C_HEADS)]
        keys = lambda kb: slice(kb * MLA_TK, (kb + 1) * MLA_TK)
        causal = {}
        for kb, masked in enumerate(sub_blocks):
            if masked:
                qpos = qi * MLA_TQ + lax.broadcasted_iota(jnp.int32, (MLA_TQ, MLA_TK), 0)
                kpos = (ki * MLA_KSPLIT + kb) * MLA_TK + lax.broadcasted_iota(jnp.int32, (MLA_TQ, MLA_TK), 1)
                causal[kb] = kpos <= qpos

        def scores(item, slot):
            kb, h = item
            s = _dot_nt(q_ref[h], k_ref[h, keys(kb), :])
            if kb in causal:
                s = jnp.where(causal[kb], s, -jnp.inf)
            s_bufs[slot][...] = s

        def softmax(item, slot):
            h = item[1]
            s_ref, p_ref = s_bufs[slot], p_bufs[slot]
            for rb in range(MLA_TQ // SOFTMAX_ROWS):
                rows = slice(rb * SOFTMAX_ROWS, (rb + 1) * SOFTMAX_ROWS)
                chunks = [slice(c * LANES, (c + 1) * LANES) for c in range(MLA_TK // LANES)]
                part = s_ref[rows, chunks[0]]
                for cols in chunks[1:]:
                    part = jnp.maximum(part, s_ref[rows, cols])
                m_prev = m_ref[h, rows, :]
                m_new = jnp.maximum(m_prev, jnp.max(part, axis=-1, keepdims=True))
                alpha = jnp.exp2(m_prev - m_new)
                lsum = alpha * l_ref[h, rows, :]
                for cols in chunks:
                    p = jnp.exp2(s_ref[rows, cols] - m_new)
                    lsum = lsum + p
                    p_ref[rows, cols] = p.astype(BF16)
                l_ref[h, rows, :] = lsum
                m_ref[h, rows, :] = m_new
                a_bufs[slot][rows, :] = alpha

        def pv(item, slot):
            kb, h = item
            j = h // 2
            keep = lo if h % 2 == 0 else jnp.logical_not(lo)
            acc = acc_ref[j]
            pv_h = _dot(p_bufs[slot][...], v_ref[j, keys(kb), :])
            acc_ref[j] = jnp.where(keep, acc * a_bufs[slot][...] + pv_h, acc)

        for t in range(len(items) + 2):
            if t < len(items):
                scores(items[t], t % 2)
            if 1 <= t <= len(items):
                softmax(items[t - 1], (t - 1) % 2)
            if t >= 2:
                pv(items[t - 2], t % 2)

    for kind, sub_blocks in enumerate(FLASH_STEP_KINDS):
        @pl.when(kind_ref[step] == kind)
        def _():
            sweep(sub_blocks)

    @pl.when(last_ref[step] == 1)
    def _():
        cols = []
        for j in range(C_HEADS // 2):
            l_even = jnp.sum(l_ref[2 * j], axis=-1, keepdims=True)
            l_odd = jnp.sum(l_ref[2 * j + 1], axis=-1, keepdims=True)
            cols.append((acc_ref[j] / jnp.where(lo, l_even, l_odd)).astype(BF16))
        o_ref[...] = x_ref[...] + _dot(jnp.concatenate(cols, axis=1), wo_ref[...])


def _flash_schedule(s):
    assert MLA_TQ == MLA_TK
    qi, ki, kind, last = [], [], [], []
    for a in range(s // MLA_TQ):
        nk = a // MLA_KSPLIT + 1
        for b in range(nk):
            subs = [b * MLA_KSPLIT + kb for kb in range(MLA_KSPLIT) if b * MLA_KSPLIT + kb <= a]
            qi.append(a)
            ki.append(b)
            kind.append(FLASH_STEP_KINDS.index(tuple(g == a for g in subs)))
            last.append(int(b == nk - 1))
    return tuple(jnp.asarray(v, jnp.int32) for v in (qi, ki, kind, last))


def _c_flash(q, k, v, x, wo):
    s = x.shape[0]
    qi, ki, kind, last = _flash_schedule(s)
    key_rows = MLA_TK * MLA_KSPLIT
    grid_spec = pltpu.PrefetchScalarGridSpec(
        num_scalar_prefetch=4,
        grid=(int(qi.shape[0]),),
        in_specs=[
            pl.BlockSpec((C_HEADS, MLA_TQ, LANES), lambda t, qi, ki, kind, last: (0, qi[t], 0)),
            pl.BlockSpec((C_HEADS, key_rows, LANES), lambda t, qi, ki, kind, last: (0, ki[t], 0)),
            pl.BlockSpec((C_HEADS // 2, key_rows, LANES), lambda t, qi, ki, kind, last: (0, ki[t], 0)),
            pl.BlockSpec((MLA_TQ, D_MODEL), lambda t, qi, ki, kind, last: (qi[t], 0)),
            pl.BlockSpec(wo.shape, lambda t, qi, ki, kind, last: (0, 0), pipeline_mode=pl.Buffered(1)),
        ],
        out_specs=pl.BlockSpec((MLA_TQ, D_MODEL), lambda t, qi, ki, kind, last: (qi[t], 0)),
        scratch_shapes=[pltpu.VMEM((C_HEADS, MLA_TQ, LANES), F32), pltpu.VMEM((C_HEADS, MLA_TQ, LANES), F32),
                        pltpu.VMEM((C_HEADS // 2, MLA_TQ, LANES), F32),
                        pltpu.VMEM((MLA_TQ, MLA_TK), F32), pltpu.VMEM((MLA_TQ, MLA_TK), F32),
                        pltpu.VMEM((MLA_TQ, MLA_TK), BF16), pltpu.VMEM((MLA_TQ, MLA_TK), BF16),
                        pltpu.VMEM((MLA_TQ, LANES), F32), pltpu.VMEM((MLA_TQ, LANES), F32)],
    )
    return pl.pallas_call(
        _c_flash_kernel,
        out_shape=jax.ShapeDtypeStruct((s, D_MODEL), F32),
        grid_spec=grid_spec,
        compiler_params=_params(("arbitrary",)),
        name="mla_flash",
    )(qi, ki, kind, last, q, k, v, x, wo)


def _pad_lanes(v, n=LANES):
    return jnp.pad(v, (0, n - v.shape[0])).reshape(1, n)


def _mixer_c(x, g, w_down, q_a_norm, kv_a_norm, w_q_up, w_kv_up, q_norm, k_norm, w_o, tables, ffn):
    cos, sin, _ = tables
    half = C_ROPE // 2
    lat = C_Q_RANK + C_KV_RANK
    w_rope = w_down[:, lat:]
    w_rope_partner = jnp.concatenate([w_rope[:, half:], w_rope[:, :half]], axis=1)
    at_rope_lanes = lambda w: jnp.pad(w, ((0, 0), (C_NOPE, LANES - C_DQK)))
    wd = jnp.concatenate([w_down[:, :lat], at_rope_lanes(w_rope), at_rope_lanes(w_rope_partner)], axis=1).astype(BF16)
    wq = w_q_up.reshape(C_Q_RANK, C_HEADS, C_DQK)
    wq = jnp.concatenate([wq, wq[:, :, C_NOPE:C_NOPE + half], jnp.zeros((C_Q_RANK, C_HEADS, half), F32)], axis=2)
    wq = wq.reshape(C_Q_RANK, C_HEADS * LANES).astype(BF16)
    wkv = w_kv_up.reshape(C_KV_RANK, C_HEADS, C_NOPE + C_V)
    wk = jnp.pad(wkv[:, :, :C_NOPE], ((0, 0), (0, 0), (0, LANES - C_NOPE))).reshape(C_KV_RANK, C_HEADS * LANES)
    wv = wkv[:, :, C_NOPE:].reshape(C_KV_RANK, C_HEADS * C_V)
    wkv = jnp.concatenate([wk, wv], axis=1).astype(BF16)
    qn = _pad_lanes(jnp.concatenate([q_norm, q_norm[C_NOPE:C_NOPE + half]]))
    kn_rope = k_norm[C_NOPE:]
    knp = _pad_lanes(jnp.concatenate([jnp.zeros((C_NOPE,), F32), kn_rope[half:], kn_rope[:half]]))
    q, k, v = _c_proj(x, g.reshape(1, -1), wd, q_a_norm.reshape(1, -1), kv_a_norm.reshape(1, -1), wq, wkv,
                      qn, _pad_lanes(k_norm), knp, _block_ones(LANES, C_DQK), cos, sin)
    return _ffn(_c_flash(q, k, v, x, w_o.astype(BF16)), ffn)


def kernel(x, positions, mix_norm, ffn_norm, a_w_qkv, a_q_norm, a_k_norm, a_sinks, a_w_o, b_w_in, b_conv_w, b_w_out,
           c_w_down, c_q_a_norm, c_kv_a_norm, c_w_q_up, c_w_kv_up, c_q_norm, c_k_norm, c_w_o, f_w_gate_up, f_w_down):
    batch, seq, d = x.shape
    assert batch == 1 and d == D_MODEL and seq % max(ROW_TILE, ROPE_TILE, MLA_TQ, MLA_TK * MLA_KSPLIT) == 0
    depth = mix_norm.shape[0]
    pos_row = positions.reshape(1, seq)
    invf_a, groups_a, sign_a = _rope_pattern(A_ROT_DIM, 0, A_HEAD_DIM)
    invf_c, groups_c, sign_c = _rope_pattern(C_ROPE, C_NOPE, LANES)
    tab_a = _rope_tables(pos_row, invf_a, groups_a) + (sign_a,)
    tab_c = _rope_tables(pos_row, invf_c, groups_c) + (sign_c,)
    xs = x.reshape(seq, d)
    ffn_all = (ffn_norm.reshape(depth, 1, d), f_w_gate_up.astype(BF16), f_w_down.astype(BF16))
    for i in range(depth):
        kind, j = i % 3, i // 3
        ffn = (i,) + ffn_all
        if kind == 0:
            xs = _mixer_a(xs, mix_norm[i], a_w_qkv[j], a_q_norm[j], a_k_norm[j], a_sinks[j], a_w_o[j], tab_a, ffn)
        elif kind == 1:
            xs = _mixer_b(xs, mix_norm[i], b_w_in[j], b_conv_w[j], b_w_out[j], ffn)
        else:
            xs = _mixer_c(xs, mix_norm[i], c_w_down[j], c_q_a_norm[j], c_kv_a_norm[j], c_w_q_up[j], c_w_kv_up[j],
                          c_q_norm[j], c_k_norm[j], c_w_o[j], tab_c, ffn)
    return xs.reshape(batch, seq, d)
```

```python
import functools
import math

import jax
import jax.numpy as jnp
import numpy as np
from jax import lax
from jax.experimental import pallas as pl
from jax.experimental.pallas import tpu as pltpu

F32 = jnp.float32
BF16 = jnp.bfloat16

D_MODEL = 1024
ROPE_THETA = 500000.0
EPS = 1e-6
A_HEADS, A_KV_HEADS, A_HEAD_DIM, A_ROT_DIM, A_WINDOW = 16, 4, 64, 16, 128
C_HEADS, C_NOPE, C_ROPE, C_V, C_Q_RANK, C_KV_RANK = 16, 64, 32, 64, 384, 256
C_DQK = C_NOPE + C_ROPE
LOG2E = math.log2(math.e)

LANES = 128
SUBLANES = 8
VMEM_LIMIT_BYTES = 56 * 2**20

ROW_TILE = 512
ROPE_TILE = 2048
FFN_CHUNK = 256
MLA_TQ = 512
MLA_TK = 512
MLA_KSPLIT = 2
FLASH_STEP_KINDS = ((False, False), (False, True), (True,))
SOFTMAX_ROWS = 64


def _params(sem):
    return pltpu.CompilerParams(dimension_semantics=sem, vmem_limit_bytes=VMEM_LIMIT_BYTES)


def _const_spec(shape):
    return pl.BlockSpec(shape, lambda *_: (0,) * len(shape), pipeline_mode=pl.Buffered(1))


def _rms(x, g):
    ms = jnp.mean(x * x, axis=-1, keepdims=True)
    return x * lax.rsqrt(ms + EPS) * g


def _dot(a, b):
    return jnp.dot(a, b, preferred_element_type=F32)


def _dot_nt(a, b):
    return lax.dot_general(a, b, (((1,), (1,)), ((), ())), preferred_element_type=F32)


def _lane_lo():
    return lax.broadcasted_iota(jnp.int32, (1, LANES), 1) < (LANES // 2)


def _rope_apply(x, cos, sin, first_half, half):
    fwd = pltpu.roll(x, LANES - half, 1)
    bwd = pltpu.roll(x, half, 1)
    return x * cos + jnp.where(first_half, fwd, bwd) * sin


def _rope_kernel(groups, pos_ref, invf_ref, cos_ref, sin_ref):
    ang = invf_ref[...] * pos_ref[...].astype(F32)
    c, s = jnp.cos(ang), jnp.sin(ang)
    ones = jnp.ones((SUBLANES, ROPE_TILE), F32)
    zeros = jnp.zeros((SUBLANES, ROPE_TILE), F32)
    cos_rows, sin_rows = [], []
    for grp in groups:
        if grp is None:
            cos_rows.append(ones)
            sin_rows.append(zeros)
        else:
            r0, sign = grp
            cos_rows.append(c[r0:r0 + SUBLANES])
            sin_rows.append(s[r0:r0 + SUBLANES] if sign > 0 else -s[r0:r0 + SUBLANES])
    cos_ref[...] = jnp.concatenate(cos_rows, axis=0).T
    sin_ref[...] = jnp.concatenate(sin_rows, axis=0).T


def _rope_tables(pos_row, invf, groups):
    s = pos_row.shape[1]
    out = pl.BlockSpec((ROPE_TILE, LANES), lambda i: (i, 0))
    return pl.pallas_call(
        functools.partial(_rope_kernel, groups),
        out_shape=(jax.ShapeDtypeStruct((s, LANES), F32),) * 2,
        grid=(s // ROPE_TILE,),
        in_specs=[pl.BlockSpec((1, ROPE_TILE), lambda i: (0, i)), _const_spec(invf.shape)],
        out_specs=(out, out),
        compiler_params=_params(("parallel",)),
        name="rope_tables",
    )(pos_row, invf)


def _rope_pattern(rot_dim, offset, period):
    half = rot_dim // 2
    assert half % SUBLANES == 0 and offset % SUBLANES == 0 and period % SUBLANES == 0
    inv_freq = ROPE_THETA ** (-jnp.arange(0, rot_dim, 2, dtype=F32) / rot_dim)
    d = np.arange(LANES) % period - offset
    rotary = (d >= 0) & (d < rot_dim)
    sign = np.where(rotary, np.where(d < half, -1.0, 1.0), 0.0)
    groups = tuple((int(d[l] % half), int(sign[l])) if rotary[l] else None for l in range(0, LANES, SUBLANES))
    return inv_freq.reshape(half, 1), groups, jnp.asarray(sign, F32).reshape(1, LANES)


def _ffn_tile(x, g_ref, wgu_ref, wd_ref):
    h = _rms(x, g_ref[...]).astype(BF16)
    d_ff = wd_ref.shape[0]
    acc = x
    for c in range(d_ff // FFN_CHUNK):
        c0 = c * FFN_CHUNK
        gate = _dot(h, wgu_ref[:, c0:c0 + FFN_CHUNK])
        up = _dot(h, wgu_ref[:, d_ff + c0:d_ff + c0 + FFN_CHUNK])
        act = (gate * jax.nn.sigmoid(gate) * up).astype(BF16)
        acc = acc + _dot(act, wd_ref[c0:c0 + FFN_CHUNK, :])
    return acc


def _ffn_specs(ffn):
    layer = ffn[0]
    pick = lambda a: pl.BlockSpec((None,) + a.shape[1:], lambda *_: (layer,) + (0,) * (a.ndim - 1),
                                  pipeline_mode=pl.Buffered(1))
    return [pick(a) for a in ffn[1:]]


def _oproj_ffn_kernel(x_ref, a_ref, wo_ref, g_ref, wgu_ref, wd_ref, o_ref):
    o_ref[...] = _ffn_tile(x_ref[...] + _dot(a_ref[...], wo_ref[...]), g_ref, wgu_ref, wd_ref)


def _oproj_ffn(x, attn, wo, ffn):
    s = x.shape[0]
    row = pl.BlockSpec((ROW_TILE, D_MODEL), lambda i: (i, 0))
    return pl.pallas_call(
        _oproj_ffn_kernel,
        out_shape=jax.ShapeDtypeStruct((s, D_MODEL), F32),
        grid=(s // ROW_TILE,),
        in_specs=[row, pl.BlockSpec((ROW_TILE, attn.shape[1]), lambda i: (i, 0)), _const_spec(wo.shape)]
        + _ffn_specs(ffn),
        out_specs=row,
        compiler_params=_params(("parallel",)),
        name="oproj_swiglu_ffn",
    )(x, attn, wo, *ffn[1:])


def _block_ones(block, valid):
    r = np.arange(2 * LANES)
    same = (r[:, None] // block) == (r[None, :] // block)
    return jnp.asarray(same & ((r % block) < valid)[:, None], BF16)


def _a_proj_kernel(x_ref, g_ref, w_ref, gain_ref, ones_ref, cos_ref, sin_ref, sign_ref, q_ref, k_ref, v_ref, raw_ref):
    @pl.when(pl.program_id(0) == 0)
    def _():
        raw_ref[...] = jnp.zeros_like(raw_ref)

    lo = _lane_lo()
    nq = A_HEADS * A_HEAD_DIM
    nkv = A_KV_HEADS * A_HEAD_DIM
    cur = slice(0, ROW_TILE)
    qkv_cols = lambda c0, n: raw_ref[:, c0:c0 + n]
    first = sign_ref[...] < 0.0
    cos, sin = cos_ref[...], sin_ref[...]
    q_scale = A_HEAD_DIM ** -0.5 * LOG2E
    half_lanes = LANES // 2

    def split(col):
        a0 = jnp.where(lo, col, 0.0)
        b1 = jnp.where(lo, 0.0, col)
        return a0, pltpu.roll(a0, half_lanes, 1), pltpu.roll(b1, half_lanes, 1), b1

    for pr in range((nq + nkv) // (2 * LANES)):
        blk4 = qkv_cols(pr * 2 * LANES, 2 * LANES)
        ss = _dot((blk4 * blk4).astype(BF16), ones_ref[...])
        y = blk4 * lax.rsqrt(ss * (1.0 / A_HEAD_DIM) + EPS) * gain_ref[:, pr * 2 * LANES:(pr + 1) * 2 * LANES]
        for c in range(2):
            col = _rope_apply(y[:, c * LANES:(c + 1) * LANES], cos, sin, first, A_ROT_DIM // 2)
            cidx = 2 * pr + c
            if cidx < nq // LANES:
                q_ref[:, cidx * LANES:(cidx + 1) * LANES] = (col * q_scale).astype(BF16)
            else:
                kc = cidx - nq // LANES
                for t, part in enumerate(split(col)):
                    k_ref[cur, (4 * kc + t) * LANES:(4 * kc + t + 1) * LANES] = part.astype(BF16)
    vals = qkv_cols(nq + nkv, nkv)
    for c in range(nkv // LANES):
        a0, a1, b0, b1 = split(vals[:, c * LANES:(c + 1) * LANES])
        v_ref[cur, (2 * c) * LANES:(2 * c + 1) * LANES] = (a0 + a1).astype(BF16)
        v_ref[cur, (2 * c + 1) * LANES:(2 * c + 2) * LANES] = (b0 + b1).astype(BF16)
    raw_ref[...] = _dot(_rms(x_ref[...], g_ref[...]).astype(BF16), w_ref[...])


def _a_attn_kernel(sink_ref, q_ref, kp_ref, kc_ref, vp_ref, vc_ref, x_ref, wo_ref, gf_ref, wgu_ref, wd_ref, o_ref,
                   oacc_ref):
    i = pl.program_id(0)
    blk = A_WINDOW
    lo = _lane_lo()
    kcat = jnp.concatenate([kp_ref[...], kc_ref[...]], axis=0)
    vcat = jnp.concatenate([vp_ref[...], vc_ref[...]], axis=0)
    qi = lax.broadcasted_iota(jnp.int32, (2 * blk, 2 * blk), 0) & (blk - 1)
    kj = lax.broadcasted_iota(jnp.int32, (2 * blk, 2 * blk), 1)
    first_key = jnp.where(i > 0, 0, blk)
    ones = jnp.ones((2 * blk, LANES), BF16)
    for b in range(ROW_TILE // blk):
        valid = (kj > qi) & (kj <= qi + blk)
        if b == 0:
            valid = valid & (kj >= first_key)
        rows = slice(b * blk, (b + 1) * blk)
        win = slice(b * blk, (b + 2) * blk)
        for g in range(A_KV_HEADS):
            qs = jnp.concatenate([q_ref[rows, (2 * g) * LANES:(2 * g + 1) * LANES],
                                  q_ref[rows, (2 * g + 1) * LANES:(2 * g + 2) * LANES]], axis=0)
            vd = jnp.concatenate([vcat[win, g * LANES:(g + 1) * LANES], ones], axis=1)
            outs = []
            for half in range(2):
                kd = kcat[win, (2 * g + half) * LANES:(2 * g + half + 1) * LANES]
                sink = jnp.concatenate([jnp.full((blk, LANES), sink_ref[4 * g + half] * LOG2E, F32),
                                        jnp.full((blk, LANES), sink_ref[4 * g + 2 + half] * LOG2E, F32)], axis=0)
                s = jnp.where(valid, _dot_nt(qs, kd), -jnp.inf)
                s0, s1 = s[:, :LANES], s[:, LANES:]
                m = jnp.maximum(jnp.max(jnp.maximum(s0, s1), axis=-1, keepdims=True), sink)
                p = jnp.concatenate([jnp.exp2(s0 - m), jnp.exp2(s1 - m)], axis=1).astype(BF16)
                ol = _dot(p, vd)
                outs.append(ol[:, :LANES] / (ol[:, LANES:] + jnp.exp2(sink - m)))
            pair = jnp.where(lo, outs[0], outs[1])
            oacc_ref[rows, (2 * g) * LANES:(2 * g + 1) * LANES] = pair[:blk].astype(BF16)
            oacc_ref[rows, (2 * g + 1) * LANES:(2 * g + 2) * LANES] = pair[blk:].astype(BF16)

    o_ref[...] = _ffn_tile(x_ref[...] + _dot(oacc_ref[...], wo_ref[...]), gf_ref, wgu_ref, wd_ref)


def _mixer_a(x, g, w_qkv, q_norm, k_norm, sinks, w_o, tables, ffn):
    cos, sin, sign = tables
    s = x.shape[0]
    nq, nk, nv = A_HEADS * A_HEAD_DIM, 2 * A_KV_HEADS * LANES, A_KV_HEADS * LANES
    gain = jnp.concatenate([jnp.tile(q_norm, A_HEADS), jnp.tile(k_norm, A_KV_HEADS)]).reshape(1, -1)
    ones = _block_ones(A_HEAD_DIM, A_HEAD_DIM)
    w, wo = w_qkv.astype(BF16), w_o.astype(BF16)
    row = lambda n: pl.BlockSpec((ROW_TILE, n), lambda i: (i, 0))
    tiles = s // ROW_TILE
    ahead = lambda n: pl.BlockSpec((ROW_TILE, n), lambda i: (jnp.minimum(i, tiles - 1), 0))
    behind = lambda n: pl.BlockSpec((ROW_TILE, n), lambda i: (jnp.maximum(i - 1, 0), 0))
    q, k, v = pl.pallas_call(
        _a_proj_kernel,
        out_shape=(jax.ShapeDtypeStruct((s, nq), BF16), jax.ShapeDtypeStruct((s, nk), BF16),
                   jax.ShapeDtypeStruct((s, nv), BF16)),
        grid=(tiles + 1,),
        in_specs=[ahead(D_MODEL), _const_spec((1, D_MODEL)), _const_spec(w.shape), _const_spec(gain.shape),
                  _const_spec(ones.shape), behind(LANES), behind(LANES), _const_spec((1, LANES))],
        out_specs=(behind(nq), behind(nk), behind(nv)),
        scratch_shapes=[pltpu.VMEM((ROW_TILE, w.shape[1]), F32)],
        compiler_params=_params(("arbitrary",)),
        name="swa_qkv_proj",
    )(x, g.reshape(1, -1), w, gain, ones, cos, sin, sign)
    per = ROW_TILE // A_WINDOW
    prev = lambda n: pl.BlockSpec((A_WINDOW, n), lambda i: (jnp.maximum(i * per - 1, 0), 0))
    return pl.pallas_call(
        _a_attn_kernel,
        out_shape=jax.ShapeDtypeStruct((s, D_MODEL), F32),
        grid=(s // ROW_TILE,),
        in_specs=[pl.BlockSpec(memory_space=pltpu.SMEM), row(nq), prev(nk), row(nk), prev(nv), row(nv), row(D_MODEL),
                  _const_spec(wo.shape)] + _ffn_specs(ffn),
        out_specs=row(D_MODEL),
        scratch_shapes=[pltpu.VMEM((ROW_TILE, nq), BF16)],
        compiler_params=_params(("parallel",)),
        name="swa_attention_layer",
    )(sinks, q, k, k, v, v, x, wo, *ffn[1:])


def _b_kernel(x_ref, g_ref, win_ref, cw_ref, wout_ref, gf_ref, wgu_ref, wd_ref, o_ref, carry_ref):
    @pl.when(pl.program_id(0) == 0)
    def _():
        carry_ref[...] = jnp.zeros_like(carry_ref)

    x = x_ref[...]
    h = _rms(x, g_ref[...]).astype(BF16)
    bcu = _dot(h, win_ref[...])
    z = bcu[:, D_MODEL:2 * D_MODEL] * bcu[:, 2 * D_MODEL:]
    r = lax.broadcasted_iota(jnp.int32, (ROW_TILE, 1), 0)
    prev1 = carry_ref[SUBLANES - 1:SUBLANES, :]
    prev2 = carry_ref[SUBLANES - 2:SUBLANES - 1, :]
    z1 = jnp.where(r == 0, prev1, pltpu.roll(z, 1, 0))
    z2 = jnp.where(r == 0, prev2, jnp.where(r == 1, prev1, pltpu.roll(z, 2, 0)))
    y = cw_ref[0:1, :] * z2 + cw_ref[1:2, :] * z1 + cw_ref[2:3, :] * z
    carry_ref[...] = z[ROW_TILE - SUBLANES:]
    o_ref[...] = _ffn_tile(x + _dot((bcu[:, :D_MODEL] * y).astype(BF16), wout_ref[...]), gf_ref, wgu_ref, wd_ref)


def _mixer_b(x, g, w_in, conv_w, w_out, ffn):
    s = x.shape[0]
    row = pl.BlockSpec((ROW_TILE, D_MODEL), lambda i: (i, 0))
    return pl.pallas_call(
        _b_kernel,
        out_shape=jax.ShapeDtypeStruct((s, D_MODEL), F32),
        grid=(s // ROW_TILE,),
        in_specs=[row, _const_spec((1, D_MODEL)), _const_spec(w_in.shape), _const_spec(conv_w.shape),
                  _const_spec(w_out.shape)] + _ffn_specs(ffn),
        out_specs=row,
        scratch_shapes=[pltpu.VMEM((SUBLANES, D_MODEL), F32)],
        compiler_params=_params(("arbitrary",)),
        name="short_conv_layer",
    )(x, g.reshape(1, -1), w_in.astype(BF16), conv_w, w_out.astype(BF16), *ffn[1:])


def _c_proj_kernel(x_ref, g_ref, wd_ref, qan_ref, kvan_ref, wq_ref, wkv_ref, qn_ref, kn_ref, knp_ref, ones_ref,
                   cos_ref, sin_ref, q_ref, k_ref, v_ref):
    h = _rms(x_ref[...], g_ref[...]).astype(BF16)
    d = _dot(h, wd_ref[...])
    cq = _rms(d[:, :C_Q_RANK], qan_ref[...]).astype(BF16)
    lat = C_Q_RANK + C_KV_RANK
    ckv = _rms(d[:, C_Q_RANK:lat], kvan_ref[...]).astype(BF16)
    kr, krp = d[:, lat:lat + LANES], d[:, lat + LANES:]
    cos, sin = cos_ref[...], sin_ref[...]
    k_rot = (kr * kn_ref[...]) * cos + (krp * knp_ref[...]) * sin
    ss_rope = jnp.sum(kr * kr, axis=-1, keepdims=True)
    q_scale = C_DQK ** -0.5 * LOG2E
    qn2 = jnp.concatenate([qn_ref[...]] * 2, axis=1)
    for j in range(C_HEADS // 2):
        if j % 2 == 0:
            cols4 = slice(2 * j * LANES, (2 * j + 4) * LANES)
            qb4 = _dot(cq, wq_ref[:, cols4])
            kb4 = _dot(ckv, wkv_ref[:, cols4])
        half4 = slice((j % 2) * 2 * LANES, (j % 2 + 1) * 2 * LANES)
        qb, kb = qb4[:, half4], kb4[:, half4]
        ssq = _dot((qb * qb).astype(BF16), ones_ref[...])
        ssk = _dot((kb * kb).astype(BF16), ones_ref[...]) + ss_rope
        y = qb * lax.rsqrt(ssq * (1.0 / C_DQK) + EPS) * qn2
        rk = lax.rsqrt(ssk * (1.0 / C_DQK) + EPS)
        for c in range(2):
            lanes = slice(c * LANES, (c + 1) * LANES)
            yc = y[:, lanes]
            yc = yc * cos + pltpu.roll(yc, LANES - C_ROPE // 2, 1) * sin
            q_ref[2 * j + c] = (yc * q_scale).astype(BF16)
            k_ref[2 * j + c] = (rk[:, lanes] * (kb[:, lanes] * kn_ref[...] + k_rot)).astype(BF16)
    v_all = _dot(ckv, wkv_ref[:, C_HEADS * LANES:])
    for j in range(C_HEADS // 2):
        v_ref[j] = v_all[:, j * LANES:(j + 1) * LANES].astype(BF16)


def _c_proj(x, g, wd, qan, kvan, wq, wkv, qn, kn, knp, ones, cos, sin):
    s = x.shape[0]
    row = lambda n: pl.BlockSpec((ROW_TILE, n), lambda i: (i, 0))
    heads = lambda n: pl.BlockSpec((n, ROW_TILE, LANES), lambda i: (0, i, 0))
    return pl.pallas_call(
        _c_proj_kernel,
        out_shape=(jax.ShapeDtypeStruct((C_HEADS, s, LANES), BF16), jax.ShapeDtypeStruct((C_HEADS, s, LANES), BF16),
                   jax.ShapeDtypeStruct((C_HEADS // 2, s, LANES), BF16)),
        grid=(s // ROW_TILE,),
        in_specs=[row(D_MODEL), _const_spec((1, D_MODEL)), _const_spec(wd.shape), _const_spec(qan.shape),
                  _const_spec(kvan.shape), _const_spec(wq.shape), _const_spec(wkv.shape), _const_spec((1, LANES)),
                  _const_spec((1, LANES)), _const_spec((1, LANES)), _const_spec(ones.shape), row(LANES), row(LANES)],
        out_specs=(heads(C_HEADS), heads(C_HEADS), heads(C_HEADS // 2)),
        compiler_params=_params(("parallel",)),
        name="mla_proj",
    )(x, g, wd, qan, kvan, wq, wkv, qn, kn, knp, ones, cos, sin)


def _c_flash_kernel(qi_ref, ki_ref, kind_ref, last_ref, q_ref, k_ref, v_ref, o_ref,
                    m_ref, l_ref, acc_ref, s0_ref, s1_ref, p0_ref, p1_ref, a0_ref, a1_ref):
    step = pl.program_id(0)
    qi, ki = qi_ref[step], ki_ref[step]
    lo = _lane_lo()
    s_bufs, p_bufs, a_bufs = (s0_ref, s1_ref), (p0_ref, p1_ref), (a0_ref, a1_ref)

    @pl.when(ki == 0)
    def _():
        m_ref[...] = jnp.full_like(m_ref, -jnp.inf)
        l_ref[...] = jnp.zeros_like(l_ref)
        acc_ref[...] = jnp.zeros_like(acc_ref)

    def sweep(sub_blocks):
        items = [(kb, h) for kb in range(len(sub_blocks)) for h in range(C_HEADS)]
        keys = lambda kb: slice(kb * MLA_TK, (kb + 1) * MLA_TK)
        causal = {}
        for kb, masked in enumerate(sub_blocks):
            if masked:
                qpos = qi * MLA_TQ + lax.broadcasted_iota(jnp.int32, (MLA_TQ, MLA_TK), 0)
                kpos = (ki * MLA_KSPLIT + kb) * MLA_TK + lax.broadcasted_iota(jnp.int32, (MLA_TQ, MLA_TK), 1)
                causal[kb] = kpos <= qpos

        def scores(item, slot):
            kb, h = item
            s = _dot_nt(q_ref[h], k_ref[h, keys(kb), :])
            if kb in causal:
                s = jnp.where(causal[kb], s, -jnp.inf)
            s_bufs[slot][...] = s

        def softmax(item, slot):
            h = item[1]
            s_ref, p_ref = s_bufs[slot], p_bufs[slot]
            for rb in range(MLA_TQ // SOFTMAX_ROWS):
                rows = slice(rb * SOFTMAX_ROWS, (rb + 1) * SOFTMAX_ROWS)
                chunks = [slice(c * LANES, (c + 1) * LANES) for c in range(MLA_TK // LANES)]
                part = s_ref[rows, chunks[0]]
                for cols in chunks[1:]:
                    part = jnp.maximum(part, s_ref[rows, cols])
                m_prev = m_ref[h, rows, :]
                m_new = jnp.maximum(m_prev, jnp.max(part, axis=-1, keepdims=True))
                alpha = jnp.exp2(m_prev - m_new)
                lsum = alpha * l_ref[h, rows, :]
                for cols in chunks:
                    p = jnp.exp2(s_ref[rows, cols] - m_new)
                    lsum = lsum + p
                    p_ref[rows, cols] = p.astype(BF16)
                l_ref[h, rows, :] = lsum
                m_ref[h, rows, :] = m_new
                a_bufs[slot][rows, :] = alpha

        def pv(item, slot):
            kb, h = item
            j = h // 2
            keep = lo if h % 2 == 0 else jnp.logical_not(lo)
            acc = acc_ref[j]
            pv_h = _dot(p_bufs[slot][...], v_ref[j, keys(kb), :])
            acc_ref[j] = jnp.where(keep, acc * a_bufs[slot][...] + pv_h, acc)

        for t in range(len(items) + 2):
            if t < len(items):
                scores(items[t], t % 2)
            if 1 <= t <= len(items):
                softmax(items[t - 1], (t - 1) % 2)
            if t >= 2:
                pv(items[t - 2], t % 2)

    for kind, sub_blocks in enumerate(FLASH_STEP_KINDS):
        @pl.when(kind_ref[step] == kind)
        def _():
            sweep(sub_blocks)

    @pl.when(last_ref[step] == 1)
    def _():
        cols = []
        for j in range(C_HEADS // 2):
            l_even = jnp.sum(l_ref[2 * j], axis=-1, keepdims=True)
            l_odd = jnp.sum(l_ref[2 * j + 1], axis=-1, keepdims=True)
            cols.append((acc_ref[j] / jnp.where(lo, l_even, l_odd)).astype(BF16))
        o_ref[...] = jnp.concatenate(cols, axis=1)


def _flash_schedule(s):
    assert MLA_TQ == MLA_TK
    qi, ki, kind, last = [], [], [], []
    for a in range(s // MLA_TQ):
        nk = a // MLA_KSPLIT + 1
        for b in range(nk):
            subs = [b * MLA_KSPLIT + kb for kb in range(MLA_KSPLIT) if b * MLA_KSPLIT + kb <= a]
            qi.append(a)
            ki.append(b)
            kind.append(FLASH_STEP_KINDS.index(tuple(g == a for g in subs)))
            last.append(int(b == nk - 1))
    return tuple(jnp.asarray(v, jnp.int32) for v in (qi, ki, kind, last))


def _c_flash(q, k, v):
    s = q.shape[1]
    qi, ki, kind, last = _flash_schedule(s)
    key_rows = MLA_TK * MLA_KSPLIT
    grid_spec = pltpu.PrefetchScalarGridSpec(
        num_scalar_prefetch=4,
        grid=(int(qi.shape[0]),),
        in_specs=[
            pl.BlockSpec((C_HEADS, MLA_TQ, LANES), lambda t, qi, ki, kind, last: (0, qi[t], 0)),
            pl.BlockSpec((C_HEADS, key_rows, LANES), lambda t, qi, ki, kind, last: (0, ki[t], 0)),
            pl.BlockSpec((C_HEADS // 2, key_rows, LANES), lambda t, qi, ki, kind, last: (0, ki[t], 0)),
        ],
        out_specs=pl.BlockSpec((MLA_TQ, C_HEADS * C_V), lambda t, qi, ki, kind, last: (qi[t], 0)),
        scratch_shapes=[pltpu.VMEM((C_HEADS, MLA_TQ, LANES), F32), pltpu.VMEM((C_HEADS, MLA_TQ, LANES), F32),
                        pltpu.VMEM((C_HEADS // 2, MLA_TQ, LANES), F32),
                        pltpu.VMEM((MLA_TQ, MLA_TK), F32), pltpu.VMEM((MLA_TQ, MLA_TK), F32),
                        pltpu.VMEM((MLA_TQ, MLA_TK), BF16), pltpu.VMEM((MLA_TQ, MLA_TK), BF16),
                        pltpu.VMEM((MLA_TQ, LANES), F32), pltpu.VMEM((MLA_TQ, LANES), F32)],
    )
    return pl.pallas_call(
        _c_flash_kernel,
        out_shape=jax.ShapeDtypeStruct((s, C_HEADS * C_V), BF16),
        grid_spec=grid_spec,
        compiler_params=_params(("arbitrary",)),
        name="mla_flash",
    )(qi, ki, kind, last, q, k, v)


def _pad_lanes(v, n=LANES):
    return jnp.pad(v, (0, n - v.shape[0])).reshape(1, n)


def _mixer_c(x, g, w_down, q_a_norm, kv_a_norm, w_q_up, w_kv_up, q_norm, k_norm, w_o, tables, ffn):
    cos, sin, _ = tables
    half = C_ROPE // 2
    lat = C_Q_RANK + C_KV_RANK
    w_rope = w_down[:, lat:]
    w_rope_partner = jnp.concatenate([w_rope[:, half:], w_rope[:, :half]], axis=1)
    at_rope_lanes = lambda w: jnp.pad(w, ((0, 0), (C_NOPE, LANES - C_DQK)))
    wd = jnp.concatenate([w_down[:, :lat], at_rope_lanes(w_rope), at_rope_lanes(w_rope_partner)], axis=1).astype(BF16)
    wq = w_q_up.reshape(C_Q_RANK, C_HEADS, C_DQK)
    wq = jnp.concatenate([wq, wq[:, :, C_NOPE:C_NOPE + half], jnp.zeros((C_Q_RANK, C_HEADS, half), F32)], axis=2)
    wq = wq.reshape(C_Q_RANK, C_HEADS * LANES).astype(BF16)
    wkv = w_kv_up.reshape(C_KV_RANK, C_HEADS, C_NOPE + C_V)
    wk = jnp.pad(wkv[:, :, :C_NOPE], ((0, 0), (0, 0), (0, LANES - C_NOPE))).reshape(C_KV_RANK, C_HEADS * LANES)
    wv = wkv[:, :, C_NOPE:].reshape(C_KV_RANK, C_HEADS * C_V)
    wkv = jnp.concatenate([wk, wv], axis=1).astype(BF16)
    qn = _pad_lanes(jnp.concatenate([q_norm, q_norm[C_NOPE:C_NOPE + half]]))
    kn_rope = k_norm[C_NOPE:]
    knp = _pad_lanes(jnp.concatenate([jnp.zeros((C_NOPE,), F32), kn_rope[half:], kn_rope[:half]]))
    q, k, v = _c_proj(x, g.reshape(1, -1), wd, q_a_norm.reshape(1, -1), kv_a_norm.reshape(1, -1), wq, wkv,
                      qn, _pad_lanes(k_norm), knp, _block_ones(LANES, C_DQK), cos, sin)
    return _oproj_ffn(x, _c_flash(q, k, v), w_o.astype(BF16), ffn)


def kernel(x, positions, mix_norm, ffn_norm, a_w_qkv, a_q_norm, a_k_norm, a_sinks, a_w_o, b_w_in, b_conv_w, b_w_out,
           c_w_down, c_q_a_norm, c_kv_a_norm, c_w_q_up, c_w_kv_up, c_q_norm, c_k_norm, c_w_o, f_w_gate_up, f_w_down):
    batch, seq, d = x.shape
    assert batch == 1 and d == D_MODEL and seq % max(ROW_TILE, ROPE_TILE, MLA_TQ, MLA_TK * MLA_KSPLIT) == 0
    depth = mix_norm.shape[0]
    pos_row = positions.reshape(1, seq)
    invf_a, groups_a, sign_a = _rope_pattern(A_ROT_DIM, 0, A_HEAD_DIM)
    invf_c, groups_c, sign_c = _rope_pattern(C_ROPE, C_NOPE, LANES)
    tab_a = _rope_tables(pos_row, invf_a, groups_a) + (sign_a,)
    tab_c = _rope_tables(pos_row, invf_c, groups_c) + (sign_c,)
    xs = x.reshape(seq, d)
    ffn_all = (ffn_norm.reshape(depth, 1, d), f_w_gate_up.astype(BF16), f_w_down.astype(BF16))
    for i in range(depth):
        kind, j = i % 3, i // 3
        ffn = (i,) + ffn_all
        if kind == 0:
            xs = _mixer_a(xs, mix_norm[i], a_w_qkv[j], a_q_norm[j], a_k_norm[j], a_sinks[j], a_w_o[j], tab_a, ffn)
        elif kind == 1:
            xs = _mixer_b(xs, mix_norm[i], b_w_in[j], b_conv_w[j], b_w_out[j], ffn)
        else:
            xs = _mixer_c(xs, mix_norm[i], c_w_down[j], c_q_a_norm[j], c_kv_a_norm[j], c_w_q_up[j], c_w_kv_up[j],
                          c_q_norm[j], c_k_norm[j], c_w_o[j], tab_c, ffn)
    return xs.reshape(batch, seq, d)
```
